```python
import math
import jax, jax.numpy as jnp
from jax import lax
import numpy as np

D_MODEL = 1024
BATCH = 8
SEQ = 2048
DEPTH = 1

N_HEADS = 8
QK_NOPE_DIM = 128
QK_ROPE_DIM = 64
V_HEAD_DIM = 128
Q_LORA_RANK = 384
KV_LORA_RANK = 256
ROPE_THETA = 10000.0
Q_BLOCK = 128
POOL_WINDOWS = (2, 4, 8, 16)
POOL_GROUP = 128
POOL_WIDTH = POOL_GROUP * len(POOL_WINDOWS)
D_FF = 2816
MACARON_WEIGHT = 0.5
N_BRANCHES = 2
NORM_EPS = 1e-6
IN_OFFSETS = [
    Q_LORA_RANK,
    Q_LORA_RANK + KV_LORA_RANK,
    Q_LORA_RANK + KV_LORA_RANK + QK_ROPE_DIM,
    Q_LORA_RANK + KV_LORA_RANK + QK_ROPE_DIM + POOL_WIDTH,
]
IN_WIDTH = Q_LORA_RANK + KV_LORA_RANK + QK_ROPE_DIM + POOL_WIDTH + N_BRANCHES * D_MODEL

kernel_name = "hybrid_mla_pool_macaron_block"


def _rmsnorm(x, g):
    xf = x.astype(jnp.float32)
    y = xf * lax.rsqrt(jnp.mean(xf * xf, axis=-1, keepdims=True) + NORM_EPS)
    return (y * g.astype(jnp.float32)).astype(x.dtype)


def _swiglu(x, w_gate, w_up, w_down):
    return (jax.nn.silu(x @ w_gate) * (x @ w_up)) @ w_down


def _rope(x, cos, sin):
    half = x.shape[-1] // 2
    x1, x2 = x[..., :half], x[..., half:]
    return jnp.concatenate([x1 * cos - x2 * sin, x2 * cos + x1 * sin], axis=-1)


def _mla(c_q, c_kv, k_r, positions, q_a_norm_g, w_uq, kv_a_norm_g, w_uk, w_uv):
    b, s, _ = c_q.shape
    q = (_rmsnorm(c_q, q_a_norm_g) @ w_uq).reshape(b, s, N_HEADS, QK_NOPE_DIM + QK_ROPE_DIM)
    q_nope, q_rope = q[..., :QK_NOPE_DIM], q[..., QK_NOPE_DIM:]
    c_kv = _rmsnorm(c_kv, kv_a_norm_g)
    k_nope = (c_kv @ w_uk).reshape(b, s, N_HEADS, QK_NOPE_DIM)
    v = (c_kv @ w_uv).reshape(b, s, N_HEADS, V_HEAD_DIM)
    inv_freq = ROPE_THETA ** (-jnp.arange(0, QK_ROPE_DIM, 2, dtype=jnp.float32) / QK_ROPE_DIM)
    ang = positions.astype(jnp.float32)[..., None] * inv_freq
    cos = jnp.cos(ang).astype(c_q.dtype)
    sin = jnp.sin(ang).astype(c_q.dtype)
    k_rope = _rope(k_r, cos, sin)
    q_rope = _rope(q_rope, cos[:, :, None, :], sin[:, :, None, :])
    scale = 1.0 / math.sqrt(QK_NOPE_DIM + QK_ROPE_DIM)
    nb = s // Q_BLOCK

    def to_blocks(t):
        return t.reshape(b, nb, Q_BLOCK, *t.shape[2:]).swapaxes(0, 1)

    def attend(qs):
        qn, qr = qs
        logits = (jnp.einsum('bqhd,bkhd->bhqk', qn, k_nope)
                  + jnp.einsum('bqhd,bkd->bhqk', qr, k_rope))
        p = jax.nn.softmax(logits.astype(jnp.float32) * scale, axis=-1).astype(v.dtype)
        return jnp.einsum('bhqk,bkhd->bqhd', p, v)

    o = lax.map(attend, (to_blocks(q_nope), to_blocks(q_rope)))
    return o.swapaxes(0, 1).reshape(b, s, N_HEADS * V_HEAD_DIM)


def _centred_mean(xg, window):
    s = xg.shape[1]
    xf = xg.astype(jnp.float32)
    csum = jnp.concatenate([jnp.zeros_like(xf[:, :1]), jnp.cumsum(xf, axis=1)], axis=1)
    left = window // 2
    right = window - 1 - left
    t = jnp.arange(s)
    lo = jnp.clip(t - left, 0, s)
    hi = jnp.clip(t + right + 1, 0, s)
    total = jnp.take(csum, hi, axis=1) - jnp.take(csum, lo, axis=1)
    count = (hi - lo).astype(jnp.float32)
    return (total / count[None, :, None]).astype(xg.dtype)


def _pool_mixer(xp, pool_w, pool_scale):
    groups = []
    for gi, w in enumerate(POOL_WINDOWS):
        xg = xp[..., gi * POOL_GROUP:(gi + 1) * POOL_GROUP]
        groups.append(_centred_mean(xg, w) - xg)
    d = jnp.stack(groups, axis=2)
    y = jnp.einsum('bsgc,gcd->bsgd', d, pool_w)
    return y.reshape(xp.shape[0], xp.shape[1], POOL_WIDTH) * pool_scale


def setup_inputs(seed: int = 0) -> dict:
    key = jax.random.key(seed)
    ks = list(jax.random.split(key, 32))

    def dense(k, shape, fan_in):
        return jax.random.normal(k, shape, jnp.float32) * (fan_in ** -0.5)

    def gain(k, shape):
        return 1.0 + 0.05 * jax.random.normal(k, shape, jnp.float32)

    L = DEPTH
    return {
        "x": jax.random.normal(ks[0], (BATCH, SEQ, D_MODEL), jnp.float32),
        "positions": jnp.broadcast_to(jnp.arange(SEQ, dtype=jnp.int32), (BATCH, SEQ)),
        "ffn1_pre_g": gain(ks[1], (L, D_MODEL)),
        "ffn1_w_gate": dense(ks[2], (L, D_MODEL, D_FF), D_MODEL),
        "ffn1_w_up": dense(ks[3], (L, D_MODEL, D_FF), D_MODEL),
        "ffn1_w_down": dense(ks[4], (L, D_FF, D_MODEL), D_FF),
        "ffn1_post_g": gain(ks[5], (L, D_MODEL)),
        "mix_pre_g": gain(ks[6], (L, D_MODEL)),
        "w_in": dense(ks[7], (L, D_MODEL, IN_WIDTH), D_MODEL),
        "q_a_norm_g": gain(ks[8], (L, Q_LORA_RANK)),
        "w_uq": dense(ks[9], (L, Q_LORA_RANK, N_HEADS * (QK_NOPE_DIM + QK_ROPE_DIM)), Q_LORA_RANK),
        "kv_a_norm_g": gain(ks[10], (L, KV_LORA_RANK)),
        "w_uk": dense(ks[11], (L, KV_LORA_RANK, N_HEADS * QK_NOPE_DIM), KV_LORA_RANK),
        "w_uv": dense(ks[12], (L, KV_LORA_RANK, N_HEADS * V_HEAD_DIM), KV_LORA_RANK),
        "w_o_attn": dense(ks[13], (L, N_HEADS * V_HEAD_DIM, D_MODEL), N_HEADS * V_HEAD_DIM),
        "pool_w": dense(ks[14], (L, len(POOL_WINDOWS), POOL_GROUP, POOL_GROUP), POOL_GROUP),
        "pool_scale": gain(ks[15], (L, POOL_WIDTH)),
        "w_o_pool": dense(ks[16], (L, POOL_WIDTH, D_MODEL), POOL_WIDTH),
        "w_out": dense(ks[17], (L, D_MODEL, D_MODEL), D_MODEL),
        "mix_post_g": gain(ks[18], (L, D_MODEL)),
        "ffn2_pre_g": gain(ks[19], (L, D_MODEL)),
        "ffn2_w_gate": dense(ks[20], (L, D_MODEL, D_FF), D_MODEL),
        "ffn2_w_up": dense(ks[21], (L, D_MODEL, D_FF), D_MODEL),
        "ffn2_w_down": dense(ks[22], (L, D_FF, D_MODEL), D_FF),
        "ffn2_post_g": gain(ks[23], (L, D_MODEL)),
        "final_g": gain(ks[24], (L, D_MODEL)),
    }


def reference(x, positions, ffn1_pre_g, ffn1_w_gate, ffn1_w_up, ffn1_w_down, ffn1_post_g,
              mix_pre_g, w_in, q_a_norm_g, w_uq, kv_a_norm_g, w_uk, w_uv, w_o_attn,
              pool_w, pool_scale, w_o_pool, w_out, mix_post_g,
              ffn2_pre_g, ffn2_w_gate, ffn2_w_up, ffn2_w_down, ffn2_post_g, final_g):
    for l in range(DEPTH):
        f1 = _swiglu(_rmsnorm(x, ffn1_pre_g[l]), ffn1_w_gate[l], ffn1_w_up[l], ffn1_w_down[l])
        x = x + MACARON_WEIGHT * _rmsnorm(f1, ffn1_post_g[l])

        u = _rmsnorm(x, mix_pre_g[l])
        z = u @ w_in[l]
        c_q, c_kv, k_r, x_pool, gate_logits = jnp.split(z, IN_OFFSETS, axis=-1)
        y_attn = _mla(c_q, c_kv, k_r, positions, q_a_norm_g[l], w_uq[l],
                      kv_a_norm_g[l], w_uk[l], w_uv[l]) @ w_o_attn[l]
        y_pool = _pool_mixer(x_pool, pool_w[l], pool_scale[l]) @ w_o_pool[l]
        g_attn, g_pool = jnp.split(jax.nn.sigmoid(gate_logits), N_BRANCHES, axis=-1)
        mixed = (g_attn * y_attn + g_pool * y_pool) @ w_out[l]
        x = x + _rmsnorm(mixed, mix_post_g[l])

        f2 = _swiglu(_rmsnorm(x, ffn2_pre_g[l]), ffn2_w_gate[l], ffn2_w_up[l], ffn2_w_down[l])
        x = x + MACARON_WEIGHT * _rmsnorm(f2, ffn2_post_g[l])

        x = _rmsnorm(x, final_g[l])
    return x
```

```python
import functools
import math

import jax
import jax.numpy as jnp
from jax.experimental import pallas as pl
from jax.experimental.pallas import tpu as pltpu

D_MODEL = 1024
N_HEADS = 8
QK_NOPE_DIM = 128
QK_ROPE_DIM = 64
QK_DIM = QK_NOPE_DIM + QK_ROPE_DIM
V_HEAD_DIM = 128
Q_LORA_RANK = 384
KV_LORA_RANK = 256
ROPE_THETA = 10000.0
POOL_WINDOWS = (2, 4, 8, 16)
POOL_GROUP = 128
POOL_WIDTH = POOL_GROUP * len(POOL_WINDOWS)
D_FF = 2816
MACARON_WEIGHT = 0.5
NORM_EPS = 1e-6

LANES = 128
SUBLANES = 8
POOL_HALO = 8

OFF_CQ = 0
OFF_CKV = OFF_CQ + Q_LORA_RANK
OFF_POOL = OFF_CKV + KV_LORA_RANK
OFF_GATE = OFF_POOL + POOL_WIDTH
OFF_KR = OFF_GATE + 2 * D_MODEL
IN_WIDTH_EXT = OFF_KR + 2 * QK_ROPE_DIM

FFN_TM = 256
MIX_TM = 256
ATT_TQ = 512
VMEM_LIMIT = 56 * 1024 * 1024


def _resident(shape):
    return pl.BlockSpec(shape, lambda *_: (0,) * len(shape), pipeline_mode=pl.Buffered(1))


def _rmsnorm(x, g):
    ms = jnp.mean(x * x, axis=-1, keepdims=True)
    return x * jax.lax.rsqrt(ms + NORM_EPS) * g


def _bdot(a, b):
    return jnp.dot(a.astype(jnp.bfloat16), b, preferred_element_type=jnp.float32)


def _ffn_kernel(x_ref, pre_g_ref, wg_ref, wu_ref, wd_ref, post_g_ref, final_g_ref, o_ref, *, final_norm):
    x = x_ref[...]
    xn = _rmsnorm(x, pre_g_ref[...]).astype(jnp.bfloat16)
    g = jnp.dot(xn, wg_ref[...], preferred_element_type=jnp.float32)
    u = jnp.dot(xn, wu_ref[...], preferred_element_type=jnp.float32)
    h = g * (1.0 / (1.0 + jnp.exp(-g))) * u
    f = _bdot(h, wd_ref[...])
    y = x + MACARON_WEIGHT * _rmsnorm(f, post_g_ref[...])
    if final_norm:
        y = _rmsnorm(y, final_g_ref[...])
    o_ref[...] = y


def _ffn(x2d, pre_g, wg, wu, wd, post_g, final_g, final_norm):
    n, d = x2d.shape
    tm = FFN_TM
    row = pl.BlockSpec((tm, d), lambda i: (i, 0))
    return pl.pallas_call(
        functools.partial(_ffn_kernel, final_norm=final_norm),
        grid=(n // tm,),
        in_specs=[row, _resident((1, d)), _resident(wg.shape), _resident(wu.shape), _resident(wd.shape),
                  _resident((1, d)), _resident((1, d))],
        out_specs=row,
        out_shape=jax.ShapeDtypeStruct((n, d), jnp.float32),
        compiler_params=pltpu.CompilerParams(dimension_semantics=("arbitrary",), vmem_limit_bytes=VMEM_LIMIT),
        name="ffn_final" if final_norm else "ffn",
    )(x2d, pre_g, wg, wu, wd, post_g, final_g)


def _mixer_in_kernel(x_ref, pos_ref, pre_g_ref, w_in_ref, qg_ref, wq_ref, wqr_ref, wqs_ref, kvg_ref, wuk_ref, wuv_ref,
                     invf_ref, q_ref, k_ref, v_ref, pool_ref, gate_ref):
    x = x_ref[0]
    u = _rmsnorm(x, pre_g_ref[...])
    z = _bdot(u, w_in_ref[...])

    ang = pos_ref[0].astype(jnp.float32) * invf_ref[...]
    cos = jnp.cos(ang)
    sin = jnp.sin(ang)
    lane = jax.lax.broadcasted_iota(jnp.int32, cos.shape, 1)
    first_half = (lane % QK_ROPE_DIM) < (QK_ROPE_DIM // 2)
    ssin = jnp.where(first_half, -sin, sin)

    kr2 = z[:, OFF_KR:OFF_KR + LANES] * jnp.where(lane < QK_ROPE_DIM, cos, ssin)
    k_rope = (kr2[:, :QK_ROPE_DIM] + kr2[:, QK_ROPE_DIM:]).astype(jnp.bfloat16)

    ckv = _rmsnorm(z[:, OFF_CKV:OFF_CKV + KV_LORA_RANK], kvg_ref[...]).astype(jnp.bfloat16)
    k_nope = jnp.dot(ckv, wuk_ref[...], preferred_element_type=jnp.float32).astype(jnp.bfloat16)
    v = jnp.dot(ckv, wuv_ref[...], preferred_element_type=jnp.float32).astype(jnp.bfloat16)

    scale = 1.0 / math.sqrt(QK_DIM)
    cq = _rmsnorm(z[:, OFF_CQ:OFF_CQ + Q_LORA_RANK], qg_ref[...]).astype(jnp.bfloat16)
    q_nope = (jnp.dot(cq, wq_ref[...], preferred_element_type=jnp.float32) * scale).astype(jnp.bfloat16)
    qr = jnp.dot(cq, wqr_ref[...], preferred_element_type=jnp.float32)
    qs = jnp.dot(cq, wqs_ref[...], preferred_element_type=jnp.float32)
    n_rep = N_HEADS * QK_ROPE_DIM // LANES
    cos_h = jnp.concatenate([cos] * n_rep, axis=1)
    ssin_h = jnp.concatenate([ssin] * n_rep, axis=1)
    q_rope = ((qr * cos_h + qs * ssin_h) * scale).astype(jnp.bfloat16)

    for h in range(N_HEADS):
        q_ref[0, h, :, 0:QK_NOPE_DIM] = q_nope[:, h * QK_NOPE_DIM:(h + 1) * QK_NOPE_DIM]
        q_ref[0, h, :, QK_NOPE_DIM:QK_DIM] = q_rope[:, h * QK_ROPE_DIM:(h + 1) * QK_ROPE_DIM]
        k_ref[0, h, :, 0:QK_NOPE_DIM] = k_nope[:, h * QK_NOPE_DIM:(h + 1) * QK_NOPE_DIM]
        k_ref[0, h, :, QK_NOPE_DIM:QK_DIM] = k_rope
        v_ref[0, h] = v[:, h * V_HEAD_DIM:(h + 1) * V_HEAD_DIM]

    pool_ref[0] = z[:, OFF_POOL:OFF_POOL + POOL_WIDTH]
    gl = z[:, OFF_GATE:OFF_GATE + 2 * D_MODEL]
    gate_ref[0] = (1.0 / (1.0 + jnp.exp(-gl))).astype(jnp.bfloat16)


def _mixer_in(x, pos3, pre_g, w_in, qg, wq, wqr, wqs, kvg, wuk, wuv, invf):
    b, s, d = x.shape
    tm = MIX_TM
    heads = lambda w: pl.BlockSpec((1, N_HEADS, tm, w), lambda bi, i: (bi, 0, i, 0))
    rows = lambda w: pl.BlockSpec((1, tm, w), lambda bi, i: (bi, i, 0))
    return pl.pallas_call(
        _mixer_in_kernel,
        grid=(b, s // tm),
        in_specs=[rows(d), rows(1), _resident((1, d)), _resident(w_in.shape), _resident(qg.shape),
                  _resident(wq.shape), _resident(wqr.shape), _resident(wqs.shape), _resident(kvg.shape),
                  _resident(wuk.shape), _resident(wuv.shape), _resident(invf.shape)],
        out_specs=[heads(QK_DIM), heads(QK_DIM), heads(V_HEAD_DIM), rows(POOL_WIDTH), rows(2 * D_MODEL)],
        out_shape=[jax.ShapeDtypeStruct((b, N_HEADS, s, QK_DIM), jnp.bfloat16),
                   jax.ShapeDtypeStruct((b, N_HEADS, s, QK_DIM), jnp.bfloat16),
                   jax.ShapeDtypeStruct((b, N_HEADS, s, V_HEAD_DIM), jnp.bfloat16),
                   jax.ShapeDtypeStruct((b, s, POOL_WIDTH), jnp.float32),
                   jax.ShapeDtypeStruct((b, s, 2 * D_MODEL), jnp.bfloat16)],
        compiler_params=pltpu.CompilerParams(dimension_semantics=("arbitrary", "arbitrary"),
                                             vmem_limit_bytes=VMEM_LIMIT),
        name="mixer_in",
    )(x, pos3, pre_g, w_in, qg, wq, wqr, wqs, kvg, wuk, wuv, invf)


def _attn_kernel(q_ref, k_ref, v_ref, o_ref):
    s = jax.lax.dot_general(q_ref[0, 0], k_ref[0, 0], (((1,), (1,)), ((), ())),
                            preferred_element_type=jnp.float32)
    m = jnp.max(s, axis=-1, keepdims=True)
    p = jnp.exp(s - m)
    l = jnp.sum(p, axis=-1, keepdims=True)
    o = jnp.dot(p.astype(jnp.bfloat16), v_ref[0, 0], preferred_element_type=jnp.float32)
    o_ref[0] = (o / l).astype(jnp.bfloat16)


def _attention(q, k, v):
    b, h, s, _ = q.shape
    tq = ATT_TQ
    return pl.pallas_call(
        _attn_kernel,
        grid=(b, h, s // tq),
        in_specs=[pl.BlockSpec((1, 1, tq, QK_DIM), lambda bi, hi, qi: (bi, hi, qi, 0)),
                  pl.BlockSpec((1, 1, s, QK_DIM), lambda bi, hi, qi: (bi, hi, 0, 0)),
                  pl.BlockSpec((1, 1, s, V_HEAD_DIM), lambda bi, hi, qi: (bi, hi, 0, 0))],
        out_specs=pl.BlockSpec((1, tq, V_HEAD_DIM), lambda bi, hi, qi: (bi, qi, hi)),
        out_shape=jax.ShapeDtypeStruct((b, s, h * V_HEAD_DIM), jnp.bfloat16),
        compiler_params=pltpu.CompilerParams(dimension_semantics=("arbitrary", "arbitrary", "arbitrary"),
                                             vmem_limit_bytes=VMEM_LIMIT),
        name="attention",
    )(q, k, v)


def _mixer_out_kernel(o_ref, pool_ref, prev_ref, next_ref, gate_ref, x_ref, wo_ref, pw_ref, ps_ref, wop_ref,
                      wout_ref, post_g_ref, out_ref, ext_ref, *, seq_len):
    i = pl.program_id(1)
    n_i = pl.num_programs(1)
    tm = pool_ref.shape[1]

    ext_ref[0:POOL_HALO, :] = jnp.where(i > 0, prev_ref[0], 0.0)
    ext_ref[POOL_HALO:POOL_HALO + tm, :] = pool_ref[0]
    ext_ref[POOL_HALO + tm:POOL_HALO + tm + POOL_HALO, :] = jnp.where(i < n_i - 1, next_ref[0], 0.0)

    t = i * tm + jax.lax.broadcasted_iota(jnp.int32, (tm, 1), 0)
    ys = []
    for gi, w in enumerate(POOL_WINDOWS):
        left = w // 2
        right = w - 1 - left
        cols = slice(gi * POOL_GROUP, (gi + 1) * POOL_GROUP)
        total = ext_ref[POOL_HALO - left:POOL_HALO - left + tm, cols]
        for j in range(-left + 1, right + 1):
            total = total + ext_ref[POOL_HALO + j:POOL_HALO + j + tm, cols]
        count = (jnp.minimum(t + right + 1, seq_len) - jnp.maximum(t - left, 0)).astype(jnp.float32)
        dg = total / count - ext_ref[POOL_HALO:POOL_HALO + tm, cols]
        ys.append(_bdot(dg, pw_ref[gi]))
    y = jnp.concatenate(ys, axis=1) * ps_ref[...]
    y_pool = _bdot(y, wop_ref[...])

    y_attn = jnp.dot(o_ref[0], wo_ref[...], preferred_element_type=jnp.float32)
    g_attn = gate_ref[0, :, 0:D_MODEL].astype(jnp.float32)
    g_pool = gate_ref[0, :, D_MODEL:2 * D_MODEL].astype(jnp.float32)
    mixed = _bdot(g_attn * y_attn + g_pool * y_pool, wout_ref[...])
    out_ref[0] = x_ref[0] + _rmsnorm(mixed, post_g_ref[...])


def _mixer_out(o, pool, gate, x, wo, pw, ps, wop, wout, post_g):
    b, s, d = x.shape
    tm = MIX_TM
    nb = tm // POOL_HALO
    last = s // POOL_HALO - 1
    rows = lambda w: pl.BlockSpec((1, tm, w), lambda bi, i: (bi, i, 0))
    prev = pl.BlockSpec((1, POOL_HALO, POOL_WIDTH), lambda bi, i: (bi, jnp.maximum(i * nb - 1, 0), 0))
    nxt = pl.BlockSpec((1, POOL_HALO, POOL_WIDTH), lambda bi, i: (bi, jnp.minimum((i + 1) * nb, last), 0))
    return pl.pallas_call(
        functools.partial(_mixer_out_kernel, seq_len=s),
        grid=(b, s // tm),
        in_specs=[rows(d), rows(POOL_WIDTH), prev, nxt, rows(2 * d), rows(d), _resident(wo.shape),
                  _resident(pw.shape), _resident(ps.shape), _resident(wop.shape), _resident(wout.shape),
                  _resident(post_g.shape)],
        out_specs=rows(d),
        out_shape=jax.ShapeDtypeStruct((b, s, d), jnp.float32),
        scratch_shapes=[pltpu.VMEM((tm + 2 * POOL_HALO, POOL_WIDTH), jnp.float32)],
        compiler_params=pltpu.CompilerParams(dimension_semantics=("arbitrary", "arbitrary"),
                                             vmem_limit_bytes=VMEM_LIMIT),
        name="mixer_out",
    )(o, pool, pool, pool, gate, x, wo, pw, ps, wop, wout, post_g)


def _swap_halves(w):
    half = w.shape[-1] // 2
    return jnp.concatenate([w[..., half:], w[..., :half]], axis=-1)


def _split_w_uq(w_uq):
    r = w_uq.shape[0]
    w3 = w_uq.reshape(r, N_HEADS, QK_DIM)
    nope = w3[:, :, :QK_NOPE_DIM].reshape(r, N_HEADS * QK_NOPE_DIM)
    rope = w3[:, :, QK_NOPE_DIM:]
    return nope, rope.reshape(r, N_HEADS * QK_ROPE_DIM), _swap_halves(rope).reshape(r, N_HEADS * QK_ROPE_DIM)


def _permute_w_in(w_in):
    kr0 = Q_LORA_RANK + KV_LORA_RANK
    kr = w_in[:, kr0:kr0 + QK_ROPE_DIM]
    return jnp.concatenate([w_in[:, :kr0], w_in[:, kr0 + QK_ROPE_DIM:], kr, _swap_halves(kr)], axis=1)


def kernel(x, positions, ffn1_pre_g, ffn1_w_gate, ffn1_w_up, ffn1_w_down, ffn1_post_g, mix_pre_g, w_in, q_a_norm_g, w_uq, kv_a_norm_g, w_uk, w_uv, w_o_attn, pool_w, pool_scale, w_o_pool, w_out, mix_post_g, ffn2_pre_g, ffn2_w_gate, ffn2_w_up, ffn2_w_down, ffn2_post_g, final_g):
    b, s, d = x.shape
    bf = lambda w: w.astype(jnp.bfloat16)
    row = lambda g: g.reshape(1, -1)
    inv_freq = ROPE_THETA ** (-jnp.arange(0, QK_ROPE_DIM, 2, dtype=jnp.float32) / QK_ROPE_DIM)
    invf = jnp.tile(inv_freq, LANES // (QK_ROPE_DIM // 2)).reshape(1, LANES)

    for l in range(w_in.shape[0]):
        x1 = _ffn(x.reshape(b * s, d), row(ffn1_pre_g[l]), bf(ffn1_w_gate[l]), bf(ffn1_w_up[l]), bf(ffn1_w_down[l]),
                  row(ffn1_post_g[l]), row(final_g[l]), False).reshape(b, s, d)
        wq, wqr, wqs = _split_w_uq(bf(w_uq[l]))
        q, k, v, pool, gate = _mixer_in(
            x1, positions.reshape(b, s, 1), row(mix_pre_g[l]), _permute_w_in(bf(w_in[l])), row(q_a_norm_g[l]),
            wq, wqr, wqs, row(kv_a_norm_g[l]), bf(w_uk[l]), bf(w_uv[l]), invf)
        o = _attention(q, k, v)
        x2 = _mixer_out(o, pool, gate, x1, bf(w_o_attn[l]), bf(pool_w[l]), row(pool_scale[l]), bf(w_o_pool[l]),
                        bf(w_out[l]), row(mix_post_g[l]))
        x = _ffn(x2.reshape(b * s, d), row(ffn2_pre_g[l]), bf(ffn2_w_gate[l]), bf(ffn2_w_up[l]), bf(ffn2_w_down[l]),
                 row(ffn2_post_g[l]), row(final_g[l]), True).reshape(b, s, d)
    return x
```

```python
import functools
import math

import jax
import jax.numpy as jnp
from jax.experimental import pallas as pl
from jax.experimental.pallas import tpu as pltpu

D_MODEL = 1024
N_HEADS = 8
QK_NOPE_DIM = 128
QK_ROPE_DIM = 64
QK_DIM = QK_NOPE_DIM + QK_ROPE_DIM
V_HEAD_DIM = 128
Q_LORA_RANK = 384
KV_LORA_RANK = 256
ROPE_THETA = 10000.0
POOL_WINDOWS = (2, 4, 8, 16)
POOL_GROUP = 128
POOL_WIDTH = POOL_GROUP * len(POOL_WINDOWS)
D_FF = 2816
MACARON_WEIGHT = 0.5
NORM_EPS = 1e-6

LANES = 128
SUBLANES = 8
POOL_HALO = 8

OFF_CQ = 0
OFF_CKV = OFF_CQ + Q_LORA_RANK
OFF_POOL = OFF_CKV + KV_LORA_RANK
OFF_GATE = OFF_POOL + POOL_WIDTH
OFF_KR = OFF_GATE + 2 * D_MODEL
IN_WIDTH_EXT = OFF_KR + 2 * QK_ROPE_DIM

FFN_TM = 256
MIX_TM = 256
ATT_TQ = 512
ATT_KC = 256
VMEM_LIMIT = 56 * 1024 * 1024


def _resident(shape):
    return pl.BlockSpec(shape, lambda *_: (0,) * len(shape), pipeline_mode=pl.Buffered(1))


def _rmsnorm(x, g):
    ms = jnp.mean(x * x, axis=-1, keepdims=True)
    return x * jax.lax.rsqrt(ms + NORM_EPS) * g


def _bdot(a, b):
    return jnp.dot(a.astype(jnp.bfloat16), b, preferred_element_type=jnp.float32)


def _ffn_kernel(x_ref, pre_g_ref, wg_ref, wu_ref, wd_ref, post_g_ref, final_g_ref, o_ref, *, final_norm):
    x = x_ref[...]
    xn = _rmsnorm(x, pre_g_ref[...]).astype(jnp.bfloat16)
    g = jnp.dot(xn, wg_ref[...], preferred_element_type=jnp.float32)
    u = jnp.dot(xn, wu_ref[...], preferred_element_type=jnp.float32)
    h = g * (1.0 / (1.0 + jnp.exp(-g))) * u
    f = _bdot(h, wd_ref[...])
    y = x + MACARON_WEIGHT * _rmsnorm(f, post_g_ref[...])
    if final_norm:
        y = _rmsnorm(y, final_g_ref[...])
    o_ref[...] = y


def _ffn(x2d, pre_g, wg, wu, wd, post_g, final_g, final_norm):
    n, d = x2d.shape
    tm = FFN_TM
    row = pl.BlockSpec((tm, d), lambda i: (i, 0))
    return pl.pallas_call(
        functools.partial(_ffn_kernel, final_norm=final_norm),
        grid=(n // tm,),
        in_specs=[row, _resident((1, d)), _resident(wg.shape), _resident(wu.shape), _resident(wd.shape),
                  _resident((1, d)), _resident((1, d))],
        out_specs=row,
        out_shape=jax.ShapeDtypeStruct((n, d), jnp.float32),
        compiler_params=pltpu.CompilerParams(dimension_semantics=("arbitrary",), vmem_limit_bytes=VMEM_LIMIT),
        name="ffn_final" if final_norm else "ffn",
    )(x2d, pre_g, wg, wu, wd, post_g, final_g)


def _mixer_in_kernel(x_ref, pos_ref, pre_g_ref, w_in_ref, qg_ref, wq_ref, wqr_ref, wqs_ref, kvg_ref, wuk_ref, wuvt_ref,
                     invf_ref, q_ref, k_ref, vt_ref, pool_ref, gate_ref):
    x = x_ref[0]
    u = _rmsnorm(x, pre_g_ref[...])
    z = _bdot(u, w_in_ref[...])

    ang = pos_ref[0].astype(jnp.float32) * invf_ref[...]
    cos = jnp.cos(ang)
    sin = jnp.sin(ang)
    lane = jax.lax.broadcasted_iota(jnp.int32, cos.shape, 1)
    first_half = (lane % QK_ROPE_DIM) < (QK_ROPE_DIM // 2)
    ssin = jnp.where(first_half, -sin, sin)

    kr2 = z[:, OFF_KR:OFF_KR + LANES] * jnp.where(lane < QK_ROPE_DIM, cos, ssin)
    k_rope = (kr2[:, :QK_ROPE_DIM] + kr2[:, QK_ROPE_DIM:]).astype(jnp.bfloat16)

    ckv = _rmsnorm(z[:, OFF_CKV:OFF_CKV + KV_LORA_RANK], kvg_ref[...]).astype(jnp.bfloat16)
    k_nope = jnp.dot(ckv, wuk_ref[...], preferred_element_type=jnp.float32).astype(jnp.bfloat16)
    vt_ref[0] = jax.lax.dot_general(wuvt_ref[...], ckv, (((1,), (1,)), ((), ())),
                                    preferred_element_type=jnp.float32).astype(jnp.bfloat16)

    scale = math.log2(math.e) / math.sqrt(QK_DIM)
    cq = _rmsnorm(z[:, OFF_CQ:OFF_CQ + Q_LORA_RANK], qg_ref[...]).astype(jnp.bfloat16)
    q_nope = (jnp.dot(cq, wq_ref[...], preferred_element_type=jnp.float32) * scale).astype(jnp.bfloat16)
    qr = jnp.dot(cq, wqr_ref[...], preferred_element_type=jnp.float32)
    qs = jnp.dot(cq, wqs_ref[...], preferred_element_type=jnp.float32)
    n_rep = N_HEADS * QK_ROPE_DIM // LANES
    cos_h = jnp.concatenate([cos] * n_rep, axis=1)
    ssin_h = jnp.concatenate([ssin] * n_rep, axis=1)
    q_rope = ((qr * cos_h + qs * ssin_h) * scale).astype(jnp.bfloat16)

    for h in range(N_HEADS):
        q_ref[0, h, :, 0:QK_NOPE_DIM] = q_nope[:, h * QK_NOPE_DIM:(h + 1) * QK_NOPE_DIM]
        q_ref[0, h, :, QK_NOPE_DIM:QK_DIM] = q_rope[:, h * QK_ROPE_DIM:(h + 1) * QK_ROPE_DIM]
        k_ref[0, h, :, 0:QK_NOPE_DIM] = k_nope[:, h * QK_NOPE_DIM:(h + 1) * QK_NOPE_DIM]
        k_ref[0, h, :, QK_NOPE_DIM:QK_DIM] = k_rope

    pool_ref[0] = z[:, OFF_POOL:OFF_POOL + POOL_WIDTH]
    gl = z[:, OFF_GATE:OFF_GATE + 2 * D_MODEL]
    gate_ref[0] = (1.0 / (1.0 + jnp.exp(-gl))).astype(jnp.bfloat16)


def _mixer_in(x, pos3, pre_g, w_in, qg, wq, wqr, wqs, kvg, wuk, wuv, invf):
    b, s, d = x.shape
    tm = MIX_TM
    heads = lambda w: pl.BlockSpec((1, N_HEADS, tm, w), lambda bi, i: (bi, 0, i, 0))
    rows = lambda w: pl.BlockSpec((1, tm, w), lambda bi, i: (bi, i, 0))
    return pl.pallas_call(
        _mixer_in_kernel,
        grid=(b, s // tm),
        in_specs=[rows(d), rows(1), _resident((1, d)), _resident(w_in.shape), _resident(qg.shape),
                  _resident(wq.shape), _resident(wqr.shape), _resident(wqs.shape), _resident(kvg.shape),
                  _resident(wuk.shape), _resident(wuv.shape), _resident(invf.shape)],
        out_specs=[heads(QK_DIM), heads(QK_DIM), pl.BlockSpec((1, N_HEADS * V_HEAD_DIM, tm), lambda bi, i: (bi, 0, i)), rows(POOL_WIDTH), rows(2 * D_MODEL)],
        out_shape=[jax.ShapeDtypeStruct((b, N_HEADS, s, QK_DIM), jnp.bfloat16),
                   jax.ShapeDtypeStruct((b, N_HEADS, s, QK_DIM), jnp.bfloat16),
                   jax.ShapeDtypeStruct((b, N_HEADS * V_HEAD_DIM, s), jnp.bfloat16),
                   jax.ShapeDtypeStruct((b, s, POOL_WIDTH), jnp.float32),
                   jax.ShapeDtypeStruct((b, s, 2 * D_MODEL), jnp.bfloat16)],
        compiler_params=pltpu.CompilerParams(dimension_semantics=("arbitrary", "arbitrary"),
                                             vmem_limit_bytes=VMEM_LIMIT),
        name="mixer_in",
    )(x, pos3, pre_g, w_in, qg, wq, wqr, wqs, kvg, wuk, wuv, invf)


def _attn_kernel(q_ref, k_ref, vt_ref, o_ref, st_ref, p_ref):
    s_len = k_ref.shape[2]
    n_sub = s_len // ATT_TQ
    n_chunk = s_len // ATT_KC

    def logits(j):
        slot = j % 2
        st = jax.lax.dot_general(k_ref[0, 0], q_ref[0, 0, j * ATT_TQ:(j + 1) * ATT_TQ, :],
                                 (((1,), (1,)), ((), ())), preferred_element_type=jnp.float32)
        st_ref[slot] = st
        return jnp.max(st, axis=0, keepdims=True)

    m_next = logits(0)
    for j in range(n_sub):
        slot = j % 2
        m = m_next
        if j + 1 < n_sub:
            m_next = logits(j + 1)
        l = jnp.zeros_like(m)
        for c in range(n_chunk):
            keys = slice(c * ATT_KC, (c + 1) * ATT_KC)
            p = jnp.exp2(st_ref[slot, keys, :] - m)
            l = l + jnp.sum(p, axis=0, keepdims=True)
            p_ref[slot, keys, :] = p.astype(jnp.bfloat16)
        ot = jnp.dot(vt_ref[0, 0], p_ref[slot], preferred_element_type=jnp.float32)
        o_ref[0, j * ATT_TQ:(j + 1) * ATT_TQ, :] = (ot / l).T.astype(jnp.bfloat16)


def _attention(q, k, vt):
    b, h, s, _ = q.shape
    head = lambda r, c: pl.BlockSpec((1, 1, r, c), lambda bi, hi: (bi, hi, 0, 0))
    return pl.pallas_call(
        _attn_kernel,
        grid=(b, h),
        in_specs=[head(s, QK_DIM), head(s, QK_DIM), head(V_HEAD_DIM, s)],
        out_specs=pl.BlockSpec((1, s, V_HEAD_DIM), lambda bi, hi: (bi, 0, hi)),
        out_shape=jax.ShapeDtypeStruct((b, s, h * V_HEAD_DIM), jnp.bfloat16),
        scratch_shapes=[pltpu.VMEM((2, s, ATT_TQ), jnp.float32), pltpu.VMEM((2, s, ATT_TQ), jnp.bfloat16)],
        compiler_params=pltpu.CompilerParams(dimension_semantics=("arbitrary", "arbitrary"),
                                             vmem_limit_bytes=VMEM_LIMIT),
        name="attention",
    )(q, k, vt)


def _mixer_out_kernel(o_ref, pool_ref, prev_ref, next_ref, gate_ref, x_ref, wo_ref, pw_ref, ps_ref, wop_ref,
                      wout_ref, post_g_ref, out_ref, ext_ref, *, seq_len):
    i = pl.program_id(1)
    n_i = pl.num_programs(1)
    tm = pool_ref.shape[1]

    ext_ref[0:POOL_HALO, :] = jnp.where(i > 0, prev_ref[0], 0.0)
    ext_ref[POOL_HALO:POOL_HALO + tm, :] = pool_ref[0]
    ext_ref[POOL_HALO + tm:POOL_HALO + tm + POOL_HALO, :] = jnp.where(i < n_i - 1, next_ref[0], 0.0)

    t = i * tm + jax.lax.broadcasted_iota(jnp.int32, (tm, 1), 0)
    ys = []
    for gi, w in enumerate(POOL_WINDOWS):
        left = w // 2
        right = w - 1 - left
        cols = slice(gi * POOL_GROUP, (gi + 1) * POOL_GROUP)
        total = ext_ref[POOL_HALO - left:POOL_HALO - left + tm, cols]
        for j in range(-left + 1, right + 1):
            total = total + ext_ref[POOL_HALO + j:POOL_HALO + j + tm, cols]
        count = (jnp.minimum(t + right + 1, seq_len) - jnp.maximum(t - left, 0)).astype(jnp.float32)
        dg = total / count - ext_ref[POOL_HALO:POOL_HALO + tm, cols]
        ys.append(_bdot(dg, pw_ref[gi]))
    y = jnp.concatenate(ys, axis=1) * ps_ref[...]
    y_pool = _bdot(y, wop_ref[...])

    y_attn = jnp.dot(o_ref[0], wo_ref[...], preferred_element_type=jnp.float32)
    g_attn = gate_ref[0, :, 0:D_MODEL].astype(jnp.float32)
    g_pool = gate_ref[0, :, D_MODEL:2 * D_MODEL].astype(jnp.float32)
    mixed = _bdot(g_attn * y_attn + g_pool * y_pool, wout_ref[...])
    out_ref[0] = x_ref[0] + _rmsnorm(mixed, post_g_ref[...])


def _mixer_out(o, pool, gate, x, wo, pw, ps, wop, wout, post_g):
    b, s, d = x.shape
    tm = MIX_TM
    nb = tm // POOL_HALO
    last = s // POOL_HALO - 1
    rows = lambda w: pl.BlockSpec((1, tm, w), lambda bi, i: (bi, i, 0))
    prev = pl.BlockSpec((1, POOL_HALO, POOL_WIDTH), lambda bi, i: (bi, jnp.maximum(i * nb - 1, 0), 0))
    nxt = pl.BlockSpec((1, POOL_HALO, POOL_WIDTH), lambda bi, i: (bi, jnp.minimum((i + 1) * nb, last), 0))
    return pl.pallas_call(
        functools.partial(_mixer_out_kernel, seq_len=s),
        grid=(b, s // tm),
        in_specs=[rows(d), rows(POOL_WIDTH), prev, nxt, rows(2 * d), rows(d), _resident(wo.shape),
                  _resident(pw.shape), _resident(ps.shape), _resident(wop.shape), _resident(wout.shape),
                  _resident(post_g.shape)],
        out_specs=rows(d),
        out_shape=jax.ShapeDtypeStruct((b, s, d), jnp.float32),
        scratch_shapes=[pltpu.VMEM((tm + 2 * POOL_HALO, POOL_WIDTH), jnp.float32)],
        compiler_params=pltpu.CompilerParams(dimension_semantics=("arbitrary", "arbitrary"),
                                             vmem_limit_bytes=VMEM_LIMIT),
        name="mixer_out",
    )(o, pool, pool, pool, gate, x, wo, pw, ps, wop, wout, post_g)


def _swap_halves(w):
    half = w.shape[-1] // 2
    return jnp.concatenate([w[..., half:], w[..., :half]], axis=-1)


def _split_w_uq(w_uq):
    r = w_uq.shape[0]
    w3 = w_uq.reshape(r, N_HEADS, QK_DIM)
    nope = w3[:, :, :QK_NOPE_DIM].reshape(r, N_HEADS * QK_NOPE_DIM)
    rope = w3[:, :, QK_NOPE_DIM:]
    return nope, rope.reshape(r, N_HEADS * QK_ROPE_DIM), _swap_halves(rope).reshape(r, N_HEADS * QK_ROPE_DIM)


def _permute_w_in(w_in):
    kr0 = Q_LORA_RANK + KV_LORA_RANK
    kr = w_in[:, kr0:kr0 + QK_ROPE_DIM]
    return jnp.concatenate([w_in[:, :kr0], w_in[:, kr0 + QK_ROPE_DIM:], kr, _swap_halves(kr)], axis=1)


def kernel(x, positions, ffn1_pre_g, ffn1_w_gate, ffn1_w_up, ffn1_w_down, ffn1_post_g, mix_pre_g, w_in, q_a_norm_g, w_uq, kv_a_norm_g, w_uk, w_uv, w_o_attn, pool_w, pool_scale, w_o_pool, w_out, mix_post_g, ffn2_pre_g, ffn2_w_gate, ffn2_w_up, ffn2_w_down, ffn2_post_g, final_g):
    b, s, d = x.shape
    bf = lambda w: w.astype(jnp.bfloat16)
    row = lambda g: g.reshape(1, -1)
    inv_freq = ROPE_THETA ** (-jnp.arange(0, QK_ROPE_DIM, 2, dtype=jnp.float32) / QK_ROPE_DIM)
    invf = jnp.tile(inv_freq, LANES // (QK_ROPE_DIM // 2)).reshape(1, LANES)

    for l in range(w_in.shape[0]):
        x1 = _ffn(x.reshape(b * s, d), row(ffn1_pre_g[l]), bf(ffn1_w_gate[l]), bf(ffn1_w_up[l]), bf(ffn1_w_down[l]),
                  row(ffn1_post_g[l]), row(final_g[l]), False).reshape(b, s, d)
        wq, wqr, wqs = _split_w_uq(bf(w_uq[l]))
        q, k, vt, pool, gate = _mixer_in(
            x1, positions.reshape(b, s, 1), row(mix_pre_g[l]), _permute_w_in(bf(w_in[l])), row(q_a_norm_g[l]),
            wq, wqr, wqs, row(kv_a_norm_g[l]), bf(w_uk[l]), bf(w_uv[l]).T, invf)
        o = _attention(q, k, vt.reshape(b, N_HEADS, V_HEAD_DIM, s))
        x2 = _mixer_out(o, pool, gate, x1, bf(w_o_attn[l]), bf(pool_w[l]), row(pool_scale[l]), bf(w_o_pool[l]),
                        bf(w_out[l]), row(mix_post_g[l]))
        x = _ffn(x2.reshape(b * s, d), row(ffn2_pre_g[l]), bf(ffn2_w_gate[l]), bf(ffn2_w_up[l]), bf(ffn2_w_down[l]),
                 row(ffn2_post_g[l]), row(final_g[l]), True).reshape(b, s, d)
    return x
```

```python
import functools
import math

import jax
import jax.numpy as jnp
from jax.experimental import pallas as pl
from jax.experimental.pallas import tpu as pltpu

D_MODEL = 1024
N_HEADS = 8
QK_NOPE_DIM = 128
QK_ROPE_DIM = 64
QK_DIM = QK_NOPE_DIM + QK_ROPE_DIM
V_HEAD_DIM = 128
Q_LORA_RANK = 384
KV_LORA_RANK = 256
ROPE_THETA = 10000.0
POOL_WINDOWS = (2, 4, 8, 16)
POOL_GROUP = 128
POOL_WIDTH = POOL_GROUP * len(POOL_WINDOWS)
D_FF = 2816
MACARON_WEIGHT = 0.5
NORM_EPS = 1e-6

LANES = 128
SUBLANES = 8
POOL_HALO = 8

OFF_CQ = 0
OFF_CKV = OFF_CQ + Q_LORA_RANK
OFF_KR = OFF_CKV + KV_LORA_RANK
OFF_POOL = OFF_KR + 2 * QK_ROPE_DIM
OFF_GATE = OFF_POOL + POOL_WIDTH
IN_WIDTH_EXT = OFF_GATE + 2 * D_MODEL
GATE_CHUNK = 256
W_IN_ROWS = 64
ROPE_TABLE_ROWS = 512

FFN_TM = 1024
FFN_SUB = 256
FFN_W_STEPS = 11
MIX_IN_TM = 512
MIX_OUT_TM = 1024
MIX_SUB = 256
ATT_TQ = 512
ATT_KC = 256
ATT_HEADS = 2
VMEM_LIMIT = 56 * 1024 * 1024


def _resident(shape):
    return pl.BlockSpec(shape, lambda *_: (0,) * len(shape), pipeline_mode=pl.Buffered(1))


def _rmsnorm(x, g):
    ms = jnp.mean(x * x, axis=-1, keepdims=True)
    return x * jax.lax.rsqrt(ms + NORM_EPS) * g


def _bdot(a, b):
    return jnp.dot(a.astype(jnp.bfloat16), b, preferred_element_type=jnp.float32)


def _ffn_kernel(x_ref, pre_g_ref, wg_ref, wu_ref, wd_ref, post_g_ref, final_g_ref, o_ref, wg_s, wu_s, wd_s, *,
                final_norm):
    i = pl.program_id(0)
    fc = wg_ref.shape[1]
    for c in range(FFN_W_STEPS):
        @pl.when(i == c)
        def _(c=c):
            cols = slice(c * fc, (c + 1) * fc)
            wg_s[:, cols] = wg_ref[...].astype(jnp.bfloat16)
            wu_s[:, cols] = wu_ref[...].astype(jnp.bfloat16)
            wd_s[cols, :] = wd_ref[...].astype(jnp.bfloat16)

    @pl.when(i >= FFN_W_STEPS)
    def _():
        for r in range(x_ref.shape[0] // FFN_SUB):
            rows = slice(r * FFN_SUB, (r + 1) * FFN_SUB)
            x = x_ref[rows, :]
            xn = _rmsnorm(x, pre_g_ref[...]).astype(jnp.bfloat16)
            g = jnp.dot(xn, wg_s[...], preferred_element_type=jnp.float32)
            u = jnp.dot(xn, wu_s[...], preferred_element_type=jnp.float32)
            h = g * (1.0 / (1.0 + jnp.exp(-g))) * u
            f = _bdot(h, wd_s[...])
            y = x + MACARON_WEIGHT * _rmsnorm(f, post_g_ref[...])
            if final_norm:
                y = _rmsnorm(y, final_g_ref[...])
            o_ref[rows, :] = y


def _ffn(x2d, pre_g, wg, wu, wd, post_g, final_g, final_norm):
    n, d = x2d.shape
    f = wg.shape[1]
    tm = FFN_TM
    fc = f // FFN_W_STEPS
    row = pl.BlockSpec((tm, d), lambda i: (jnp.maximum(i - FFN_W_STEPS, 0), 0))
    chunk = lambda i: jnp.minimum(i, FFN_W_STEPS - 1)
    return pl.pallas_call(
        functools.partial(_ffn_kernel, final_norm=final_norm),
        grid=(FFN_W_STEPS + n // tm,),
        in_specs=[row, _resident((1, d)),
                  pl.BlockSpec((d, fc), lambda i: (0, chunk(i))), pl.BlockSpec((d, fc), lambda i: (0, chunk(i))),
                  pl.BlockSpec((fc, d), lambda i: (chunk(i), 0)), _resident((1, d)), _resident((1, d))],
        out_specs=row,
        out_shape=jax.ShapeDtypeStruct((n, d), jnp.float32),
        scratch_shapes=[pltpu.VMEM((d, f), jnp.bfloat16), pltpu.VMEM((d, f), jnp.bfloat16),
                        pltpu.VMEM((f, d), jnp.bfloat16)],
        compiler_params=pltpu.CompilerParams(dimension_semantics=("arbitrary",), vmem_limit_bytes=VMEM_LIMIT),
        name="ffn_final" if final_norm else "ffn",
    )(x2d, pre_g, wg, wu, wd, post_g, final_g)


def _rope_table_kernel(pos_ref, invf_ref, cos_ref, sin_ref):
    ang = pos_ref[...].astype(jnp.float32) * invf_ref[...]
    n_freq = QK_ROPE_DIM // 2
    tok_per_row = LANES // n_freq
    rows = ang.shape[0]
    for table, out_ref in ((jnp.cos(ang), cos_ref), (jnp.sin(ang), sin_ref)):
        for k in range(tok_per_row):
            piece = table[:, k * n_freq:(k + 1) * n_freq]
            out_ref[pl.ds(k, rows, stride=tok_per_row), :] = jnp.concatenate([piece] * tok_per_row, axis=1)


def _rope_tables(pos_rep, invf):
    n, _ = pos_rep.shape
    tr = ROPE_TABLE_ROWS
    tok_per_row = LANES // (QK_ROPE_DIM // 2)
    blk = pl.BlockSpec((tr, LANES), lambda i: (i, 0))
    out_blk = pl.BlockSpec((tr * tok_per_row, LANES), lambda i: (i, 0))
    return pl.pallas_call(
        _rope_table_kernel,
        grid=(n // tr,),
        in_specs=[blk, _resident(invf.shape)],
        out_specs=[out_blk, out_blk],
        out_shape=[jax.ShapeDtypeStruct((n * tok_per_row, LANES), jnp.float32)] * 2,
        compiler_params=pltpu.CompilerParams(dimension_semantics=("arbitrary",)),
        name="rope_tables",
    )(pos_rep, invf)


def _extend_w_in(w_ref, wext_ref):
    kr1 = OFF_KR + QK_ROPE_DIM
    half = QK_ROPE_DIM // 2
    n_in = w_ref.shape[1]

    def body(rb, carry):
        rows = pl.ds(pl.multiple_of(rb * W_IN_ROWS, W_IN_ROWS), W_IN_ROWS)
        wext_ref[rows, 0:kr1] = w_ref[rows, 0:kr1].astype(jnp.bfloat16)
        wext_ref[rows, kr1:kr1 + half] = w_ref[rows, OFF_KR + half:kr1].astype(jnp.bfloat16)
        wext_ref[rows, kr1 + half:OFF_POOL] = w_ref[rows, OFF_KR:OFF_KR + half].astype(jnp.bfloat16)
        wext_ref[rows, OFF_POOL:IN_WIDTH_EXT] = w_ref[rows, kr1:n_in].astype(jnp.bfloat16)
        return carry

    jax.lax.fori_loop(0, w_ref.shape[0] // W_IN_ROWS, body, 0)


def _mixer_in_kernel(x_ref, cos_ref, sin_ref, pre_g_ref, w_in_ref, *refs):
    weights, (q_ref, k_ref, vt_ref, pool_ref, gate_ref), wext_ref = refs[:-6], refs[-6:-1], refs[-1]

    @pl.when((pl.program_id(0) == 0) & (pl.program_id(1) == 0))
    def _():
        _extend_w_in(w_in_ref, wext_ref)

    weights = (pre_g_ref, wext_ref) + tuple(weights)
    for r in range(x_ref.shape[1] // MIX_SUB):
        rows = pl.ds(r * MIX_SUB, MIX_SUB)
        _mixer_in_tile(x_ref.at[:, rows, :], cos_ref.at[:, rows, :], sin_ref.at[:, rows, :], *weights,
                       q_ref.at[:, :, rows, :], k_ref.at[:, :, rows, :], vt_ref.at[:, :, rows],
                       pool_ref.at[:, rows, :], gate_ref.at[:, rows, :])


def _mixer_in_tile(x_ref, cos_ref, sin_ref, pre_g_ref, w_in_ref, qg_ref, wq_ref, wqr_ref, kvg_ref, wuk_ref, wuvt_ref,
                   q_ref, k_ref, vt_ref, pool_ref, gate_ref):
    u = _rmsnorm(x_ref[0], pre_g_ref[...]).astype(jnp.bfloat16)

    def gate_chunks(first, last):
        for c in range(first, last):
            cols = slice(c * GATE_CHUNK, (c + 1) * GATE_CHUNK)
            gl = jnp.dot(u, w_in_ref[:, OFF_GATE + c * GATE_CHUNK:OFF_GATE + (c + 1) * GATE_CHUNK],
                         preferred_element_type=jnp.float32)
            gate_ref[0, :, cols] = (1.0 / (1.0 + jnp.exp(-gl))).astype(jnp.bfloat16)

    z = jnp.dot(u, w_in_ref[:, :OFF_GATE], preferred_element_type=jnp.float32)
    pool_ref[0] = z[:, OFF_POOL:OFF_POOL + POOL_WIDTH]

    gate_chunks(0, 1)
    cos = cos_ref[0]
    sin = sin_ref[0]
    gate_chunks(1, 2)
    lane = jax.lax.broadcasted_iota(jnp.int32, cos.shape, 1)
    first_half = (lane % QK_ROPE_DIM) < (QK_ROPE_DIM // 2)
    ssin = jnp.where(first_half, -sin, sin)

    gate_chunks(2, 3)
    kr2 = z[:, OFF_KR:OFF_KR + LANES] * jnp.where(lane < QK_ROPE_DIM, cos, ssin)
    k_rope = (kr2[:, :QK_ROPE_DIM] + kr2[:, QK_ROPE_DIM:]).astype(jnp.bfloat16)
    ckv = _rmsnorm(z[:, OFF_CKV:OFF_CKV + KV_LORA_RANK], kvg_ref[...]).astype(jnp.bfloat16)
    cq = _rmsnorm(z[:, OFF_CQ:OFF_CQ + Q_LORA_RANK], qg_ref[...]).astype(jnp.bfloat16)

    gate_chunks(3, 4)
    k_nope = jnp.dot(ckv, wuk_ref[...], preferred_element_type=jnp.float32).astype(jnp.bfloat16)
    for h in range(N_HEADS):
        k_ref[0, h, :, 0:QK_NOPE_DIM] = k_nope[:, h * QK_NOPE_DIM:(h + 1) * QK_NOPE_DIM]
        k_ref[0, h, :, QK_NOPE_DIM:QK_DIM] = k_rope
    vt_ref[0] = jax.lax.dot_general(wuvt_ref[...], ckv, (((1,), (1,)), ((), ())),
                                    preferred_element_type=jnp.float32).astype(jnp.bfloat16)

    scale = math.log2(math.e) / math.sqrt(QK_DIM)
    qr = jnp.dot(cq, wqr_ref[...], preferred_element_type=jnp.float32)
    gate_chunks(4, 5)
    n_rep = N_HEADS * QK_ROPE_DIM // LANES
    cos_h = jnp.concatenate([cos] * n_rep, axis=1)
    ssin_h = jnp.concatenate([ssin] * n_rep, axis=1)
    half = QK_ROPE_DIM // 2
    qs = jnp.where(jnp.concatenate([first_half] * n_rep, axis=1),
                   pltpu.roll(qr, qr.shape[1] - half, axis=1), pltpu.roll(qr, half, axis=1))
    q_rope = ((qr * cos_h + qs * ssin_h) * scale).astype(jnp.bfloat16)
    q_nope = (jnp.dot(cq, wq_ref[...], preferred_element_type=jnp.float32) * scale).astype(jnp.bfloat16)
    gate_chunks(5, 6)
    for h in range(N_HEADS):
        q_ref[0, h, :, 0:QK_NOPE_DIM] = q_nope[:, h * QK_NOPE_DIM:(h + 1) * QK_NOPE_DIM]
        q_ref[0, h, :, QK_NOPE_DIM:QK_DIM] = q_rope[:, h * QK_ROPE_DIM:(h + 1) * QK_ROPE_DIM]
    gate_chunks(6, 2 * D_MODEL // GATE_CHUNK)


def _mixer_in(x, cos, sin, pre_g, w_in, qg, wq, wqr, kvg, wuk, wuv):
    b, s, d = x.shape
    tm = MIX_IN_TM
    heads = lambda w: pl.BlockSpec((1, N_HEADS, tm, w), lambda bi, i: (bi, 0, i, 0))
    rows = lambda w: pl.BlockSpec((1, tm, w), lambda bi, i: (bi, i, 0))
    return pl.pallas_call(
        _mixer_in_kernel,
        grid=(b, s // tm),
        in_specs=[rows(d), rows(LANES), rows(LANES), _resident((1, d)), _resident(w_in.shape), _resident(qg.shape),
                  _resident(wq.shape), _resident(wqr.shape), _resident(kvg.shape),
                  _resident(wuk.shape), _resident(wuv.shape)],
        out_specs=[heads(QK_DIM), heads(QK_DIM), pl.BlockSpec((1, N_HEADS * V_HEAD_DIM, tm), lambda bi, i: (bi, 0, i)), rows(POOL_WIDTH), rows(2 * D_MODEL)],
        out_shape=[jax.ShapeDtypeStruct((b, N_HEADS, s, QK_DIM), jnp.bfloat16),
                   jax.ShapeDtypeStruct((b, N_HEADS, s, QK_DIM), jnp.bfloat16),
                   jax.ShapeDtypeStruct((b, N_HEADS * V_HEAD_DIM, s), jnp.bfloat16),
                   jax.ShapeDtypeStruct((b, s, POOL_WIDTH), jnp.float32),
                   jax.ShapeDtypeStruct((b, s, 2 * D_MODEL), jnp.bfloat16)],
        scratch_shapes=[pltpu.VMEM((d, IN_WIDTH_EXT), jnp.bfloat16)],
        compiler_params=pltpu.CompilerParams(dimension_semantics=("arbitrary", "arbitrary"),
                                             vmem_limit_bytes=VMEM_LIMIT),
        name="mixer_in",
    )(x, cos, sin, pre_g, w_in, qg, wq, wqr, kvg, wuk, wuv)


def _attn_kernel(q_ref, k_ref, vt_ref, o_ref, st_ref, p_ref):
    n_heads, s_len = k_ref.shape[1], k_ref.shape[2]
    n_sub = s_len // ATT_TQ
    n_chunk = s_len // ATT_KC
    tiles = [(hh, j) for hh in range(n_heads) for j in range(n_sub)]

    def logits(t):
        hh, j = tiles[t]
        st = jax.lax.dot_general(k_ref[0, hh], q_ref[0, hh, j * ATT_TQ:(j + 1) * ATT_TQ, :],
                                 (((1,), (1,)), ((), ())), preferred_element_type=jnp.float32)
        st_ref[t % 2] = st
        return jnp.max(st, axis=0, keepdims=True)

    m_next = logits(0)
    for t, (hh, j) in enumerate(tiles):
        slot = t % 2
        m = m_next
        if t + 1 < len(tiles):
            m_next = logits(t + 1)
        l = jnp.zeros_like(m)
        for c in range(n_chunk):
            keys = slice(c * ATT_KC, (c + 1) * ATT_KC)
            p = jnp.exp2(st_ref[slot, keys, :] - m)
            l = l + jnp.sum(p, axis=0, keepdims=True)
            p_ref[slot, keys, :] = p.astype(jnp.bfloat16)
        ot = jnp.dot(vt_ref[0, hh], p_ref[slot], preferred_element_type=jnp.float32)
        o_ref[0, j * ATT_TQ:(j + 1) * ATT_TQ, hh * V_HEAD_DIM:(hh + 1) * V_HEAD_DIM] = (
            (ot / l).T.astype(jnp.bfloat16))


def _attention(q, k, vt):
    b, h, s, _ = q.shape
    hs = ATT_HEADS
    head = lambda r, c: pl.BlockSpec((1, hs, r, c), lambda bi, hi: (bi, hi, 0, 0))
    return pl.pallas_call(
        _attn_kernel,
        grid=(b, h // hs),
        in_specs=[head(s, QK_DIM), head(s, QK_DIM), head(V_HEAD_DIM, s)],
        out_specs=pl.BlockSpec((1, s, hs * V_HEAD_DIM), lambda bi, hi: (bi, 0, hi)),
        out_shape=jax.ShapeDtypeStruct((b, s, h * V_HEAD_DIM), jnp.bfloat16),
        scratch_shapes=[pltpu.VMEM((2, s, ATT_TQ), jnp.float32), pltpu.VMEM((2, s, ATT_TQ), jnp.bfloat16)],
        compiler_params=pltpu.CompilerParams(dimension_semantics=("arbitrary", "arbitrary"),
                                             vmem_limit_bytes=VMEM_LIMIT),
        name="attention",
    )(q, k, vt)


def _mixer_out_kernel(o_ref, pool_ref, prev_ref, next_ref, gate_ref, x_ref, wo_ref, pw_ref, ps_ref, wop_ref,
                      wout_ref, post_g_ref, out_ref, ext_ref, *, seq_len):
    i = pl.program_id(1)
    n_i = pl.num_programs(1)
    tm = pool_ref.shape[1]

    ext_ref[0:POOL_HALO, :] = jnp.where(i > 0, prev_ref[0], 0.0)
    ext_ref[POOL_HALO:POOL_HALO + tm, :] = pool_ref[0]
    ext_ref[POOL_HALO + tm:POOL_HALO + tm + POOL_HALO, :] = jnp.where(i < n_i - 1, next_ref[0], 0.0)

    win = MIX_SUB + 2 * POOL_HALO
    for r in range(tm // MIX_SUB):
        r0 = r * MIX_SUB
        rows = slice(r0, r0 + MIX_SUB)
        o_rows = o_ref[0, rows, :]
        t_first = i * tm + r0 + jax.lax.broadcasted_iota(jnp.int32, (POOL_HALO, 1), 0)
        t_last = t_first + (MIX_SUB - POOL_HALO)
        ys, y_attn_parts = [], []
        n_col = D_MODEL // len(POOL_WINDOWS)
        for gi, w in enumerate(POOL_WINDOWS):
            y_attn_parts.append(jnp.dot(o_rows, wo_ref[:, gi * n_col:(gi + 1) * n_col],
                                        preferred_element_type=jnp.float32))
            left = w // 2
            right = w - 1 - left
            cols = slice(gi * POOL_GROUP, (gi + 1) * POOL_GROUP)
            xe = ext_ref[r0:r0 + win, cols]
            fwd, span = xe, 1
            while span < min(w, POOL_HALO):
                fwd = fwd + pltpu.roll(fwd, win - span, axis=0)
                span *= 2
            if w > span:
                fwd = fwd + pltpu.roll(fwd, win - span, axis=0)
            total = pltpu.roll(fwd, left, axis=0)[POOL_HALO:POOL_HALO + MIX_SUB]
            inv = [1.0 / (jnp.minimum(tt + right + 1, seq_len) - jnp.maximum(tt - left, 0)).astype(jnp.float32)
                   for tt in (t_first, t_last)]
            mean = jnp.concatenate([total[:POOL_HALO] * inv[0], total[POOL_HALO:-POOL_HALO] * (1.0 / w),
                                    total[-POOL_HALO:] * inv[1]], axis=0)
            dg = mean - xe[POOL_HALO:POOL_HALO + MIX_SUB]
            ys.append(_bdot(dg, pw_ref[gi]))
        y = jnp.concatenate(ys, axis=1) * ps_ref[...]
        y_pool = _bdot(y, wop_ref[...])
        y_attn = jnp.concatenate(y_attn_parts, axis=1)

        g_attn = gate_ref[0, rows, 0:D_MODEL].astype(jnp.float32)
        g_pool = gate_ref[0, rows, D_MODEL:2 * D_MODEL].astype(jnp.float32)
        mixed = _bdot(g_attn * y_attn + g_pool * y_pool, wout_ref[...])
        out_ref[0, rows, :] = x_ref[0, rows, :] + _rmsnorm(mixed, post_g_ref[...])


def _mixer_out(o, pool, gate, x, wo, pw, ps, wop, wout, post_g):
    b, s, d = x.shape
    tm = MIX_OUT_TM
    nb = tm // POOL_HALO
    last = s // POOL_HALO - 1
    rows = lambda w: pl.BlockSpec((1, tm, w), lambda bi, i: (bi, i, 0))
    prev = pl.BlockSpec((1, POOL_HALO, POOL_WIDTH), lambda bi, i: (bi, jnp.maximum(i * nb - 1, 0), 0))
    nxt = pl.BlockSpec((1, POOL_HALO, POOL_WIDTH), lambda bi, i: (bi, jnp.minimum((i + 1) * nb, last), 0))
    return pl.pallas_call(
        functools.partial(_mixer_out_kernel, seq_len=s),
        grid=(b, s // tm),
        in_specs=[rows(d), rows(POOL_WIDTH), prev, nxt, rows(2 * d), rows(d), _resident(wo.shape),
                  _resident(pw.shape), _resident(ps.shape), _resident(wop.shape), _resident(wout.shape),
                  _resident(post_g.shape)],
        out_specs=rows(d),
        out_shape=jax.ShapeDtypeStruct((b, s, d), jnp.float32),
        scratch_shapes=[pltpu.VMEM((tm + 2 * POOL_HALO, POOL_WIDTH), jnp.float32)],
        compiler_params=pltpu.CompilerParams(dimension_semantics=("arbitrary", "arbitrary"),
                                             vmem_limit_bytes=VMEM_LIMIT),
        name="mixer_out",
    )(o, pool, pool, pool, gate, x, wo, pw, ps, wop, wout, post_g)


def _split_w_uq(w_uq):
    r = w_uq.shape[0]
    w3 = w_uq.reshape(r, N_HEADS, QK_DIM)
    nope = w3[:, :, :QK_NOPE_DIM].reshape(r, N_HEADS * QK_NOPE_DIM)
    return nope, w3[:, :, QK_NOPE_DIM:].reshape(r, N_HEADS * QK_ROPE_DIM)


def kernel(x, positions, ffn1_pre_g, ffn1_w_gate, ffn1_w_up, ffn1_w_down, ffn1_post_g, mix_pre_g, w_in, q_a_norm_g, w_uq, kv_a_norm_g, w_uk, w_uv, w_o_attn, pool_w, pool_scale, w_o_pool, w_out, mix_post_g, ffn2_pre_g, ffn2_w_gate, ffn2_w_up, ffn2_w_down, ffn2_post_g, final_g):
    b, s, d = x.shape
    bf = lambda w: w.astype(jnp.bfloat16)
    row = lambda g: g.reshape(1, -1)
    inv_freq = ROPE_THETA ** (-jnp.arange(0, QK_ROPE_DIM, 2, dtype=jnp.float32) / QK_ROPE_DIM)
    n_freq = QK_ROPE_DIM // 2
    invf = jnp.tile(inv_freq, LANES // n_freq).reshape(1, LANES)
    pos_rep = jnp.repeat(positions.reshape(b * s * n_freq // LANES, LANES // n_freq), n_freq, axis=1)
    cos, sin = (t.reshape(b, s, LANES) for t in _rope_tables(pos_rep, invf))

    for l in range(w_in.shape[0]):
        x1 = _ffn(x.reshape(b * s, d), row(ffn1_pre_g[l]), ffn1_w_gate[l], ffn1_w_up[l], ffn1_w_down[l],
                  row(ffn1_post_g[l]), row(final_g[l]), False).reshape(b, s, d)
        wq, wqr = _split_w_uq(bf(w_uq[l]))
        q, k, vt, pool, gate = _mixer_in(
            x1, cos, sin, row(mix_pre_g[l]), w_in[l], row(q_a_norm_g[l]),
            wq, wqr, row(kv_a_norm_g[l]), bf(w_uk[l]), bf(w_uv[l]).T)
        o = _attention(q, k, vt.reshape(b, N_HEADS, V_HEAD_DIM, s))
        x2 = _mixer_out(o, pool, gate, x1, bf(w_o_attn[l]), bf(pool_w[l]), row(pool_scale[l]), bf(w_o_pool[l]),
                        bf(w_out[l]), row(mix_post_g[l]))
        x = _ffn(x2.reshape(b * s, d), row(ffn2_pre_g[l]), ffn2_w_gate[l], ffn2_w_up[l], ffn2_w_down[l],
                 row(ffn2_post_g[l]), row(final_g[l]), True).reshape(b, s, d)
    return x
```

```python
import functools
import math

import jax
import jax.numpy as jnp
from jax.experimental import pallas as pl
from jax.experimental.pallas import tpu as pltpu

D_MODEL = 1024
N_HEADS = 8
QK_NOPE_DIM = 128
QK_ROPE_DIM = 64
QK_DIM = QK_NOPE_DIM + QK_ROPE_DIM
V_HEAD_DIM = 128
Q_LORA_RANK = 384
KV_LORA_RANK = 256
ROPE_THETA = 10000.0
POOL_WINDOWS = (2, 4, 8, 16)
POOL_GROUP = 128
POOL_WIDTH = POOL_GROUP * len(POOL_WINDOWS)
D_FF = 2816
MACARON_WEIGHT = 0.5
NORM_EPS = 1e-6

LANES = 128
SUBLANES = 8
POOL_HALO = 8

OFF_CQ = 0
OFF_CKV = OFF_CQ + Q_LORA_RANK
OFF_KR = OFF_CKV + KV_LORA_RANK
OFF_POOL = OFF_KR + 2 * QK_ROPE_DIM
OFF_GATE = OFF_POOL + POOL_WIDTH
IN_WIDTH_EXT = OFF_GATE + 2 * D_MODEL
W_IN_ROWS = 64
ROPE_TABLE_ROWS = 512

FFN_TM = 1024
FFN_SUB = 256
MIX_IN_TM = 512
MIX_OUT_TM = 1024
MIX_SUB = 256
ATT_TQ = 512
ATT_KC = 256
ATT_HEADS = 2
VMEM_LIMIT = 56 * 1024 * 1024

def _resident(shape):
    return pl.BlockSpec(shape, lambda *_: (0,) * len(shape), pipeline_mode=pl.Buffered(1))


def _rmsnorm(x, g):
    ms = jnp.mean(x * x, axis=-1, keepdims=True)
    return x * jax.lax.rsqrt(ms + NORM_EPS) * g


def _bdot(a, b):
    return jnp.dot(a.astype(jnp.bfloat16), b, preferred_element_type=jnp.float32)


def _ffn_kernel(x_ref, pre_g_ref, wg_ref, wu_ref, wd_ref, post_g_ref, final_g_ref, o_ref, *, final_norm):
    for r in range(x_ref.shape[0] // FFN_SUB):
        rows = slice(r * FFN_SUB, (r + 1) * FFN_SUB)
        x = x_ref[rows, :]
        xn = _rmsnorm(x, pre_g_ref[...]).astype(jnp.bfloat16)
        g = jnp.dot(xn, wg_ref[...], preferred_element_type=jnp.float32)
        u = jnp.dot(xn, wu_ref[...], preferred_element_type=jnp.float32)
        h = g * (1.0 / (1.0 + jnp.exp(-g))) * u
        f = _bdot(h, wd_ref[...])
        y = x + MACARON_WEIGHT * _rmsnorm(f, post_g_ref[...])
        if final_norm:
            y = _rmsnorm(y, final_g_ref[...])
        o_ref[rows, :] = y


def _ffn(x2d, pre_g, wg, wu, wd, post_g, final_g, final_norm):
    n, d = x2d.shape
    tm = FFN_TM
    row = pl.BlockSpec((tm, d), lambda i: (i, 0))
    return pl.pallas_call(
        functools.partial(_ffn_kernel, final_norm=final_norm),
        grid=(n // tm,),
        in_specs=[row, _resident((1, d)), _resident(wg.shape), _resident(wu.shape), _resident(wd.shape),
                  _resident((1, d)), _resident((1, d))],
        out_specs=row,
        out_shape=jax.ShapeDtypeStruct((n, d), jnp.float32),
        compiler_params=pltpu.CompilerParams(dimension_semantics=("arbitrary",), vmem_limit_bytes=VMEM_LIMIT),
        name="ffn_final" if final_norm else "ffn",
    )(x2d, pre_g, wg, wu, wd, post_g, final_g)


def _cast_stream_specs(weights, n_steps, step_of):
    spec = lambda w: pl.BlockSpec((w.shape[0] // n_steps, w.shape[1]), lambda *g: (step_of(*g), 0))
    specs = [spec(w) for w in weights]
    return specs, specs, [jax.ShapeDtypeStruct(w.shape, jnp.bfloat16) for w in weights]


def _cast_stream_step(in_refs, out_refs):
    for src, dst in zip(in_refs, out_refs):
        dst[...] = src[...].astype(jnp.bfloat16)


def _rope_table_kernel(pos_ref, invf_ref, *refs):
    n_w = (len(refs) - 2) // 2
    cos_ref, sin_ref = refs[n_w:n_w + 2]
    _cast_stream_step(refs[:n_w], refs[n_w + 2:])
    ang = pos_ref[...].astype(jnp.float32) * invf_ref[...]
    n_freq = QK_ROPE_DIM // 2
    tok_per_row = LANES // n_freq
    rows = ang.shape[0]
    for table, out_ref in ((jnp.cos(ang), cos_ref), (jnp.sin(ang), sin_ref)):
        for k in range(tok_per_row):
            piece = table[:, k * n_freq:(k + 1) * n_freq]
            out_ref[pl.ds(k, rows, stride=tok_per_row), :] = jnp.concatenate([piece] * tok_per_row, axis=1)


def _rope_tables(pos_rep, invf, weights):
    n, _ = pos_rep.shape
    tr = ROPE_TABLE_ROWS
    tok_per_row = LANES // (QK_ROPE_DIM // 2)
    blk = pl.BlockSpec((tr, LANES), lambda i: (i, 0))
    out_blk = pl.BlockSpec((tr * tok_per_row, LANES), lambda i: (i, 0))
    w_in_specs, w_out_specs, w_shapes = _cast_stream_specs(weights, n // tr, lambda i: i)
    return pl.pallas_call(
        _rope_table_kernel,
        grid=(n // tr,),
        in_specs=[blk, _resident(invf.shape)] + w_in_specs,
        out_specs=[out_blk, out_blk] + w_out_specs,
        out_shape=[jax.ShapeDtypeStruct((n * tok_per_row, LANES), jnp.float32)] * 2 + w_shapes,
        compiler_params=pltpu.CompilerParams(dimension_semantics=("arbitrary",), vmem_limit_bytes=VMEM_LIMIT),
        name="rope_tables",
    )(pos_rep, invf, *weights)


def _extend_w_in(w_ref, wext_ref):
    kr1 = OFF_KR + QK_ROPE_DIM
    half = QK_ROPE_DIM // 2
    n_in = w_ref.shape[1]

    def body(rb, carry):
        rows = pl.ds(pl.multiple_of(rb * W_IN_ROWS, W_IN_ROWS), W_IN_ROWS)
        wext_ref[rows, 0:kr1] = w_ref[rows, 0:kr1].astype(jnp.bfloat16)
        wext_ref[rows, kr1:kr1 + half] = w_ref[rows, OFF_KR + half:kr1].astype(jnp.bfloat16)
        wext_ref[rows, kr1 + half:OFF_POOL] = w_ref[rows, OFF_KR:OFF_KR + half].astype(jnp.bfloat16)
        wext_ref[rows, OFF_POOL:IN_WIDTH_EXT] = w_ref[rows, kr1:n_in].astype(jnp.bfloat16)
        return carry

    jax.lax.fori_loop(0, w_ref.shape[0] // W_IN_ROWS, body, 0)


def _mixer_in_kernel(x_ref, cos_ref, sin_ref, pre_g_ref, w_in_ref, *refs):
    weights, (q_ref, k_ref, vt_ref, pool_ref, gate_ref), wext_ref = refs[:-6], refs[-6:-1], refs[-1]

    @pl.when((pl.program_id(0) == 0) & (pl.program_id(1) == 0))
    def _():
        _extend_w_in(w_in_ref, wext_ref)

    weights = (pre_g_ref, wext_ref) + tuple(weights)
    for r in range(x_ref.shape[1] // MIX_SUB):
        rows = pl.ds(r * MIX_SUB, MIX_SUB)
        _mixer_in_tile(x_ref.at[:, rows, :], cos_ref.at[:, rows, :], sin_ref.at[:, rows, :], *weights,
                       q_ref.at[:, :, rows, :], k_ref.at[:, :, rows, :], vt_ref.at[:, :, rows],
                       pool_ref.at[:, rows, :], gate_ref.at[:, rows, :])


def _mixer_in_tile(x_ref, cos_ref, sin_ref, pre_g_ref, w_in_ref, qg_ref, wq_ref, wqr_ref, kvg_ref, wuk_ref, wuvt_ref,
                   q_ref, k_ref, vt_ref, pool_ref, gate_ref):
    u = _rmsnorm(x_ref[0], pre_g_ref[...]).astype(jnp.bfloat16)
    z = jnp.dot(u, w_in_ref[:, :OFF_GATE], preferred_element_type=jnp.float32)
    pool_ref[0] = z[:, OFF_POOL:OFF_POOL + POOL_WIDTH]

    gl = jnp.dot(u, w_in_ref[:, OFF_GATE:], preferred_element_type=jnp.float32)
    gate_ref[0] = (1.0 / (1.0 + jnp.exp(-gl))).astype(jnp.bfloat16)

    cos = cos_ref[0]
    sin = sin_ref[0]
    lane = jax.lax.broadcasted_iota(jnp.int32, cos.shape, 1)
    first_half = (lane % QK_ROPE_DIM) < (QK_ROPE_DIM // 2)
    ssin = jnp.where(first_half, -sin, sin)

    kr2 = z[:, OFF_KR:OFF_KR + LANES] * jnp.where(lane < QK_ROPE_DIM, cos, ssin)
    k_rope = (kr2[:, :QK_ROPE_DIM] + kr2[:, QK_ROPE_DIM:]).astype(jnp.bfloat16)
    ckv = _rmsnorm(z[:, OFF_CKV:OFF_CKV + KV_LORA_RANK], kvg_ref[...]).astype(jnp.bfloat16)
    cq = _rmsnorm(z[:, OFF_CQ:OFF_CQ + Q_LORA_RANK], qg_ref[...]).astype(jnp.bfloat16)

    k_nope = jnp.dot(ckv, wuk_ref[...], preferred_element_type=jnp.float32).astype(jnp.bfloat16)
    for h in range(N_HEADS):
        k_ref[0, h, :, 0:QK_NOPE_DIM] = k_nope[:, h * QK_NOPE_DIM:(h + 1) * QK_NOPE_DIM]
        k_ref[0, h, :, QK_NOPE_DIM:QK_DIM] = k_rope
    vt_ref[0] = jax.lax.dot_general(wuvt_ref[...], ckv, (((1,), (1,)), ((), ())),
                                    preferred_element_type=jnp.float32).astype(jnp.bfloat16)

    scale = math.log2(math.e) / math.sqrt(QK_DIM)
    qr = jnp.dot(cq, wqr_ref[...], preferred_element_type=jnp.float32)
    n_rep = N_HEADS * QK_ROPE_DIM // LANES
    cos_h = jnp.concatenate([cos] * n_rep, axis=1)
    ssin_h = jnp.concatenate([ssin] * n_rep, axis=1)
    half = QK_ROPE_DIM // 2
    qs = jnp.where(jnp.concatenate([first_half] * n_rep, axis=1),
                   pltpu.roll(qr, qr.shape[1] - half, axis=1), pltpu.roll(qr, half, axis=1))
    q_rope = ((qr * cos_h + qs * ssin_h) * scale).astype(jnp.bfloat16)
    q_nope = (jnp.dot(cq, wq_ref[...], preferred_element_type=jnp.float32) * scale).astype(jnp.bfloat16)
    for h in range(N_HEADS):
        q_ref[0, h, :, 0:QK_NOPE_DIM] = q_nope[:, h * QK_NOPE_DIM:(h + 1) * QK_NOPE_DIM]
        q_ref[0, h, :, QK_NOPE_DIM:QK_DIM] = q_rope[:, h * QK_ROPE_DIM:(h + 1) * QK_ROPE_DIM]


def _mixer_in(x, cos, sin, pre_g, w_in, qg, wq, wqr, kvg, wuk, wuv):
    b, s, d = x.shape
    tm = MIX_IN_TM
    heads = lambda w: pl.BlockSpec((1, N_HEADS, tm, w), lambda bi, i: (bi, 0, i, 0))
    rows = lambda w: pl.BlockSpec((1, tm, w), lambda bi, i: (bi, i, 0))
    return pl.pallas_call(
        _mixer_in_kernel,
        grid=(b, s // tm),
        in_specs=[rows(d), rows(LANES), rows(LANES), _resident((1, d)), _resident(w_in.shape), _resident(qg.shape),
                  _resident(wq.shape), _resident(wqr.shape), _resident(kvg.shape),
                  _resident(wuk.shape), _resident(wuv.shape)],
        out_specs=[heads(QK_DIM), heads(QK_DIM), pl.BlockSpec((1, N_HEADS * V_HEAD_DIM, tm), lambda bi, i: (bi, 0, i)), rows(POOL_WIDTH), rows(2 * D_MODEL)],
        out_shape=[jax.ShapeDtypeStruct((b, N_HEADS, s, QK_DIM), jnp.bfloat16),
                   jax.ShapeDtypeStruct((b, N_HEADS, s, QK_DIM), jnp.bfloat16),
                   jax.ShapeDtypeStruct((b, N_HEADS * V_HEAD_DIM, s), jnp.bfloat16),
                   jax.ShapeDtypeStruct((b, s, POOL_WIDTH), jnp.float32),
                   jax.ShapeDtypeStruct((b, s, 2 * D_MODEL), jnp.bfloat16)],
        scratch_shapes=[pltpu.VMEM((d, IN_WIDTH_EXT), jnp.bfloat16)],
        compiler_params=pltpu.CompilerParams(dimension_semantics=("arbitrary", "arbitrary"),
                                             vmem_limit_bytes=VMEM_LIMIT),
        name="mixer_in",
    )(x, cos, sin, pre_g, w_in, qg, wq, wqr, kvg, wuk, wuv)


def _attn_kernel(q_ref, k_ref, vt_ref, o_ref, st_ref, p_ref):
    n_heads, s_len = k_ref.shape[1], k_ref.shape[2]
    n_sub = s_len // ATT_TQ
    n_chunk = s_len // ATT_KC
    tiles = [(hh, j) for hh in range(n_heads) for j in range(n_sub)]

    def logits(t):
        hh, j = tiles[t]
        st = jax.lax.dot_general(k_ref[0, hh], q_ref[0, hh, j * ATT_TQ:(j + 1) * ATT_TQ, :],
                                 (((1,), (1,)), ((), ())), preferred_element_type=jnp.float32)
        st_ref[t % 2] = st
        return jnp.max(st, axis=0, keepdims=True)

    m_next = logits(0)
    for t, (hh, j) in enumerate(tiles):
        slot = t % 2
        m = m_next
        if t + 1 < len(tiles):
            m_next = logits(t + 1)
        l = jnp.zeros_like(m)
        for c in range(n_chunk):
            keys = slice(c * ATT_KC, (c + 1) * ATT_KC)
            p = jnp.exp2(st_ref[slot, keys, :] - m)
            l = l + jnp.sum(p, axis=0, keepdims=True)
            p_ref[slot, keys, :] = p.astype(jnp.bfloat16)
        ot = jnp.dot(vt_ref[0, hh], p_ref[slot], preferred_element_type=jnp.float32)
        o_ref[0, j * ATT_TQ:(j + 1) * ATT_TQ, hh * V_HEAD_DIM:(hh + 1) * V_HEAD_DIM] = (
            (ot / l).T.astype(jnp.bfloat16))


def _attention(q, k, vt):
    b, h, s, _ = q.shape
    hs = ATT_HEADS
    head = lambda r, c: pl.BlockSpec((1, hs, r, c), lambda bi, hi: (bi, hi, 0, 0))
    return pl.pallas_call(
        _attn_kernel,
        grid=(b, h // hs),
        in_specs=[head(s, QK_DIM), head(s, QK_DIM), head(V_HEAD_DIM, s)],
        out_specs=pl.BlockSpec((1, s, hs * V_HEAD_DIM), lambda bi, hi: (bi, 0, hi)),
        out_shape=jax.ShapeDtypeStruct((b, s, h * V_HEAD_DIM), jnp.bfloat16),
        scratch_shapes=[pltpu.VMEM((2, s, ATT_TQ), jnp.float32), pltpu.VMEM((2, s, ATT_TQ), jnp.bfloat16)],
        compiler_params=pltpu.CompilerParams(dimension_semantics=("arbitrary", "arbitrary"),
                                             vmem_limit_bytes=VMEM_LIMIT),
        name="attention",
    )(q, k, vt)


def _mixer_out_kernel(o_ref, pool_ref, prev_ref, next_ref, gate_ref, x_ref, wo_ref, pw_ref, ps_ref, wop_ref,
                      wout_ref, post_g_ref, *rest, seq_len):
    n_w = (len(rest) - 2) // 2
    out_ref, ext_ref = rest[n_w], rest[-1]
    _cast_stream_step(rest[:n_w], rest[n_w + 1:-1])
    i = pl.program_id(1)
    n_i = pl.num_programs(1)
    tm = pool_ref.shape[1]

    ext_ref[0:POOL_HALO, :] = jnp.where(i > 0, prev_ref[0], 0.0)
    ext_ref[POOL_HALO:POOL_HALO + tm, :] = pool_ref[0]
    ext_ref[POOL_HALO + tm:POOL_HALO + tm + POOL_HALO, :] = jnp.where(i < n_i - 1, next_ref[0], 0.0)

    win = MIX_SUB + 2 * POOL_HALO
    for r in range(tm // MIX_SUB):
        r0 = r * MIX_SUB
        rows = slice(r0, r0 + MIX_SUB)
        o_rows = o_ref[0, rows, :]
        t_first = i * tm + r0 + jax.lax.broadcasted_iota(jnp.int32, (POOL_HALO, 1), 0)
        t_last = t_first + (MIX_SUB - POOL_HALO)
        ys, y_attn_parts = [], []
        n_col = D_MODEL // len(POOL_WINDOWS)
        for gi, w in enumerate(POOL_WINDOWS):
            y_attn_parts.append(jnp.dot(o_rows, wo_ref[:, gi * n_col:(gi + 1) * n_col],
                                        preferred_element_type=jnp.float32))
            left = w // 2
            right = w - 1 - left
            cols = slice(gi * POOL_GROUP, (gi + 1) * POOL_GROUP)
            xe = ext_ref[r0:r0 + win, cols]
            fwd, span = xe, 1
            while span < min(w, POOL_HALO):
                fwd = fwd + pltpu.roll(fwd, win - span, axis=0)
                span *= 2
            if w > span:
                fwd = fwd + pltpu.roll(fwd, win - span, axis=0)
            total = pltpu.roll(fwd, left, axis=0)[POOL_HALO:POOL_HALO + MIX_SUB]
            inv = [1.0 / (jnp.minimum(tt + right + 1, seq_len) - jnp.maximum(tt - left, 0)).astype(jnp.float32)
                   for tt in (t_first, t_last)]
            mean = jnp.concatenate([total[:POOL_HALO] * inv[0], total[POOL_HALO:-POOL_HALO] * (1.0 / w),
                                    total[-POOL_HALO:] * inv[1]], axis=0)
            dg = mean - xe[POOL_HALO:POOL_HALO + MIX_SUB]
            ys.append(_bdot(dg, pw_ref[gi]))
        y = jnp.concatenate(ys, axis=1) * ps_ref[...]
        y_pool = _bdot(y, wop_ref[...])
        y_attn = jnp.concatenate(y_attn_parts, axis=1)

        g_attn = gate_ref[0, rows, 0:D_MODEL].astype(jnp.float32)
        g_pool = gate_ref[0, rows, D_MODEL:2 * D_MODEL].astype(jnp.float32)
        mixed = _bdot(g_attn * y_attn + g_pool * y_pool, wout_ref[...])
        out_ref[0, rows, :] = x_ref[0, rows, :] + _rmsnorm(mixed, post_g_ref[...])


def _mixer_out(o, pool, gate, x, wo, pw, ps, wop, wout, post_g, weights):
    b, s, d = x.shape
    tm = MIX_OUT_TM
    nb = tm // POOL_HALO
    last = s // POOL_HALO - 1
    n_i = s // tm
    w_in_specs, w_out_specs, w_shapes = _cast_stream_specs(weights, b * n_i, lambda bi, i: bi * n_i + i)
    rows = lambda w: pl.BlockSpec((1, tm, w), lambda bi, i: (bi, i, 0))
    prev = pl.BlockSpec((1, POOL_HALO, POOL_WIDTH), lambda bi, i: (bi, jnp.maximum(i * nb - 1, 0), 0))
    nxt = pl.BlockSpec((1, POOL_HALO, POOL_WIDTH), lambda bi, i: (bi, jnp.minimum((i + 1) * nb, last), 0))
    return pl.pallas_call(
        functools.partial(_mixer_out_kernel, seq_len=s),
        grid=(b, s // tm),
        in_specs=[rows(d), rows(POOL_WIDTH), prev, nxt, rows(2 * d), rows(d), _resident(wo.shape),
                  _resident(pw.shape), _resident(ps.shape), _resident(wop.shape), _resident(wout.shape),
                  _resident(post_g.shape)] + w_in_specs,
        out_specs=[rows(d)] + w_out_specs,
        out_shape=[jax.ShapeDtypeStruct((b, s, d), jnp.float32)] + w_shapes,
        scratch_shapes=[pltpu.VMEM((tm + 2 * POOL_HALO, POOL_WIDTH), jnp.float32)],
        compiler_params=pltpu.CompilerParams(dimension_semantics=("arbitrary", "arbitrary"),
                                             vmem_limit_bytes=VMEM_LIMIT),
        name="mixer_out",
    )(o, pool, pool, pool, gate, x, wo, pw, ps, wop, wout, post_g, *weights)


def _split_w_uq(w_uq):
    r = w_uq.shape[0]
    w3 = w_uq.reshape(r, N_HEADS, QK_DIM)
    nope = w3[:, :, :QK_NOPE_DIM].reshape(r, N_HEADS * QK_NOPE_DIM)
    return nope, w3[:, :, QK_NOPE_DIM:].reshape(r, N_HEADS * QK_ROPE_DIM)


def kernel(x, positions, ffn1_pre_g, ffn1_w_gate, ffn1_w_up, ffn1_w_down, ffn1_post_g, mix_pre_g, w_in, q_a_norm_g, w_uq, kv_a_norm_g, w_uk, w_uv, w_o_attn, pool_w, pool_scale, w_o_pool, w_out, mix_post_g, ffn2_pre_g, ffn2_w_gate, ffn2_w_up, ffn2_w_down, ffn2_post_g, final_g):
    b, s, d = x.shape
    bf = lambda w: w.astype(jnp.bfloat16)
    row = lambda g: g.reshape(1, -1)
    inv_freq = ROPE_THETA ** (-jnp.arange(0, QK_ROPE_DIM, 2, dtype=jnp.float32) / QK_ROPE_DIM)
    n_freq = QK_ROPE_DIM // 2
    invf = jnp.tile(inv_freq, LANES // n_freq).reshape(1, LANES)
    pos_rep = jnp.repeat(positions.reshape(b * s * n_freq // LANES, LANES // n_freq), n_freq, axis=1)
    assert w_in.shape[0] == 1, "one layer: each FFN's weight casts are hosted by the call that precedes it"
    l = 0
    cos, sin, *ffn1_w = _rope_tables(pos_rep, invf, [ffn1_w_gate[l], ffn1_w_up[l], ffn1_w_down[l]])
    cos, sin = cos.reshape(b, s, LANES), sin.reshape(b, s, LANES)

    x1 = _ffn(x.reshape(b * s, d), row(ffn1_pre_g[l]), *ffn1_w, row(ffn1_post_g[l]), row(final_g[l]),
              False).reshape(b, s, d)
    wq, wqr = _split_w_uq(bf(w_uq[l]))
    q, k, vt, pool, gate = _mixer_in(
        x1, cos, sin, row(mix_pre_g[l]), w_in[l], row(q_a_norm_g[l]),
        wq, wqr, row(kv_a_norm_g[l]), bf(w_uk[l]), bf(w_uv[l]).T)
    o = _attention(q, k, vt.reshape(b, N_HEADS, V_HEAD_DIM, s))
    x2, *ffn2_w = _mixer_out(o, pool, gate, x1, bf(w_o_attn[l]), bf(pool_w[l]), row(pool_scale[l]),
                             bf(w_o_pool[l]), bf(w_out[l]), row(mix_post_g[l]),
                             [ffn2_w_gate[l], ffn2_w_up[l], ffn2_w_down[l]])
    return _ffn(x2.reshape(b * s, d), row(ffn2_pre_g[l]), *ffn2_w, row(ffn2_post_g[l]), row(final_g[l]),
                True).reshape(b, s, d)
```

```python
import functools
import math

import jax
import jax.numpy as jnp
from jax.experimental import pallas as pl
from jax.experimental.pallas import tpu as pltpu

D_MODEL = 1024
N_HEADS = 8
QK_NOPE_DIM = 128
QK_ROPE_DIM = 64
QK_DIM = QK_NOPE_DIM + QK_ROPE_DIM
V_HEAD_DIM = 128
Q_LORA_RANK = 384
KV_LORA_RANK = 256
ROPE_THETA = 10000.0
POOL_WINDOWS = (2, 4, 8, 16)
POOL_GROUP = 128
POOL_WIDTH = POOL_GROUP * len(POOL_WINDOWS)
D_FF = 2816
MACARON_WEIGHT = 0.5
NORM_EPS = 1e-6

LANES = 128
SUBLANES = 8
POOL_HALO = 8

OFF_CQ = 0
OFF_CKV = OFF_CQ + Q_LORA_RANK
OFF_KR = OFF_CKV + KV_LORA_RANK
OFF_POOL = OFF_KR + 2 * QK_ROPE_DIM
OFF_GATE = OFF_POOL + POOL_WIDTH
IN_WIDTH_EXT = OFF_GATE + 2 * D_MODEL
W_IN_ROWS = 64
FFN_W_STEPS = 11
ROPE_TABLE_ROWS = 512

FFN_TM = 1024
FFN_SUB = 256
MIX_IN_TM = 512
MIX_OUT_TM = 1024
MIX_SUB = 256
ATT_TQ = 512
ATT_KC = 256
ATT_HEADS = 2
VMEM_LIMIT = 56 * 1024 * 1024

def _resident(shape):
    return pl.BlockSpec(shape, lambda *_: (0,) * len(shape), pipeline_mode=pl.Buffered(1))


def _rmsnorm(x, g):
    ms = jnp.mean(x * x, axis=-1, keepdims=True)
    return x * jax.lax.rsqrt(ms + NORM_EPS) * g


def _bdot(a, b):
    return jnp.dot(a.astype(jnp.bfloat16), b, preferred_element_type=jnp.float32)


def _ffn_kernel(x_ref, pre_g_ref, wg_ref, wu_ref, wd_ref, post_g_ref, final_g_ref, o_ref, *w_scratch, final_norm,
                w_steps):
    weights = w_scratch if w_steps else (wg_ref, wu_ref, wd_ref)

    def row_tile():
        for r in range(x_ref.shape[0] // FFN_SUB):
            rows = slice(r * FFN_SUB, (r + 1) * FFN_SUB)
            x = x_ref[rows, :]
            xn = _rmsnorm(x, pre_g_ref[...]).astype(jnp.bfloat16)
            g = jnp.dot(xn, weights[0][...], preferred_element_type=jnp.float32)
            u = jnp.dot(xn, weights[1][...], preferred_element_type=jnp.float32)
            h = g * (1.0 / (1.0 + jnp.exp(-g))) * u
            f = _bdot(h, weights[2][...])
            y = x + MACARON_WEIGHT * _rmsnorm(f, post_g_ref[...])
            if final_norm:
                y = _rmsnorm(y, final_g_ref[...])
            o_ref[rows, :] = y

    if not w_steps:
        row_tile()
        return
    i = pl.program_id(0)
    fc = wg_ref.shape[1]
    for c in range(w_steps):
        @pl.when(i == c)
        def _(c=c):
            cols = slice(c * fc, (c + 1) * fc)
            weights[0][:, cols] = wg_ref[...].astype(jnp.bfloat16)
            weights[1][:, cols] = wu_ref[...].astype(jnp.bfloat16)
            weights[2][cols, :] = wd_ref[...].astype(jnp.bfloat16)
    pl.when(i >= w_steps)(row_tile)


def _ffn(x2d, pre_g, wg, wu, wd, post_g, final_g, final_norm):
    n, d = x2d.shape
    f = wg.shape[1]
    tm = FFN_TM
    w_steps = 0 if wg.dtype == jnp.bfloat16 else FFN_W_STEPS
    if w_steps:
        fc = f // w_steps
        chunk = lambda i: jnp.minimum(i, w_steps - 1)
        w_specs = [pl.BlockSpec((d, fc), lambda i: (0, chunk(i))), pl.BlockSpec((d, fc), lambda i: (0, chunk(i))),
                   pl.BlockSpec((fc, d), lambda i: (chunk(i), 0))]
        w_scratch = [pltpu.VMEM((d, f), jnp.bfloat16), pltpu.VMEM((d, f), jnp.bfloat16),
                     pltpu.VMEM((f, d), jnp.bfloat16)]
    else:
        w_specs, w_scratch = [_resident(wg.shape), _resident(wu.shape), _resident(wd.shape)], []
    row = pl.BlockSpec((tm, d), lambda i: (jnp.maximum(i - w_steps, 0), 0))
    return pl.pallas_call(
        functools.partial(_ffn_kernel, final_norm=final_norm, w_steps=w_steps),
        grid=(w_steps + n // tm,),
        in_specs=[row, _resident((1, d))] + w_specs + [_resident((1, d)), _resident((1, d))],
        out_specs=row,
        out_shape=jax.ShapeDtypeStruct((n, d), jnp.float32),
        scratch_shapes=w_scratch,
        compiler_params=pltpu.CompilerParams(dimension_semantics=("arbitrary",), vmem_limit_bytes=VMEM_LIMIT),
        name="ffn_final" if final_norm else "ffn",
    )(x2d, pre_g, wg, wu, wd, post_g, final_g)


def _cast_stream_specs(weights, n_steps, step_of):
    spec = lambda w: pl.BlockSpec((w.shape[0] // n_steps, w.shape[1]), lambda *g: (step_of(*g), 0))
    specs = [spec(w) for w in weights]
    return specs, specs, [jax.ShapeDtypeStruct(w.shape, jnp.bfloat16) for w in weights]


def _cast_stream_step(in_refs, out_refs):
    for src, dst in zip(in_refs, out_refs):
        dst[...] = src[...].astype(jnp.bfloat16)


def _rope_table_kernel(pos_ref, invf_ref, cos_ref, sin_ref):
    ang = pos_ref[...].astype(jnp.float32) * invf_ref[...]
    n_freq = QK_ROPE_DIM // 2
    tok_per_row = LANES // n_freq
    rows = ang.shape[0]
    for table, out_ref in ((jnp.cos(ang), cos_ref), (jnp.sin(ang), sin_ref)):
        for k in range(tok_per_row):
            piece = table[:, k * n_freq:(k + 1) * n_freq]
            out_ref[pl.ds(k, rows, stride=tok_per_row), :] = jnp.concatenate([piece] * tok_per_row, axis=1)


def _rope_tables(pos_rep, invf):
    n, _ = pos_rep.shape
    tr = ROPE_TABLE_ROWS
    tok_per_row = LANES // (QK_ROPE_DIM // 2)
    blk = pl.BlockSpec((tr, LANES), lambda i: (i, 0))
    out_blk = pl.BlockSpec((tr * tok_per_row, LANES), lambda i: (i, 0))
    return pl.pallas_call(
        _rope_table_kernel,
        grid=(n // tr,),
        in_specs=[blk, _resident(invf.shape)],
        out_specs=[out_blk, out_blk],
        out_shape=[jax.ShapeDtypeStruct((n * tok_per_row, LANES), jnp.float32)] * 2,
        compiler_params=pltpu.CompilerParams(dimension_semantics=("arbitrary",)),
        name="rope_tables",
    )(pos_rep, invf)


def _extend_w_in(w_ref, wext_ref):
    kr1 = OFF_KR + QK_ROPE_DIM
    half = QK_ROPE_DIM // 2
    n_in = w_ref.shape[1]

    def body(rb, carry):
        rows = pl.ds(pl.multiple_of(rb * W_IN_ROWS, W_IN_ROWS), W_IN_ROWS)
        wext_ref[rows, 0:kr1] = w_ref[rows, 0:kr1].astype(jnp.bfloat16)
        wext_ref[rows, kr1:kr1 + half] = w_ref[rows, OFF_KR + half:kr1].astype(jnp.bfloat16)
        wext_ref[rows, kr1 + half:OFF_POOL] = w_ref[rows, OFF_KR:OFF_KR + half].astype(jnp.bfloat16)
        wext_ref[rows, OFF_POOL:IN_WIDTH_EXT] = w_ref[rows, kr1:n_in].astype(jnp.bfloat16)
        return carry

    jax.lax.fori_loop(0, w_ref.shape[0] // W_IN_ROWS, body, 0)


def _mixer_in_kernel(x_ref, cos_ref, sin_ref, pre_g_ref, w_in_ref, *refs):
    weights, (q_ref, k_ref, vt_ref, pool_ref, gate_ref), wext_ref = refs[:-6], refs[-6:-1], refs[-1]

    @pl.when((pl.program_id(0) == 0) & (pl.program_id(1) == 0))
    def _():
        _extend_w_in(w_in_ref, wext_ref)

    weights = (pre_g_ref, wext_ref) + tuple(weights)
    for r in range(x_ref.shape[1] // MIX_SUB):
        rows = pl.ds(r * MIX_SUB, MIX_SUB)
        _mixer_in_tile(x_ref.at[:, rows, :], cos_ref.at[:, rows, :], sin_ref.at[:, rows, :], *weights,
                       q_ref.at[:, :, rows, :], k_ref.at[:, :, rows, :], vt_ref.at[:, :, rows],
                       pool_ref.at[:, rows, :], gate_ref.at[:, rows, :])


def _mixer_in_tile(x_ref, cos_ref, sin_ref, pre_g_ref, w_in_ref, qg_ref, wq_ref, wqr_ref, kvg_ref, wuk_ref, wuvt_ref,
                   q_ref, k_ref, vt_ref, pool_ref, gate_ref):
    u = _rmsnorm(x_ref[0], pre_g_ref[...]).astype(jnp.bfloat16)
    z = jnp.dot(u, w_in_ref[:, :OFF_GATE], preferred_element_type=jnp.float32)
    pool_ref[0] = z[:, OFF_POOL:OFF_POOL + POOL_WIDTH]

    gl = jnp.dot(u, w_in_ref[:, OFF_GATE:], preferred_element_type=jnp.float32)
    gate_ref[0] = (1.0 / (1.0 + jnp.exp(-gl))).astype(jnp.bfloat16)

    cos = cos_ref[0]
    sin = sin_ref[0]
    lane = jax.lax.broadcasted_iota(jnp.int32, cos.shape, 1)
    first_half = (lane % QK_ROPE_DIM) < (QK_ROPE_DIM // 2)
    ssin = jnp.where(first_half, -sin, sin)

    kr2 = z[:, OFF_KR:OFF_KR + LANES] * jnp.where(lane < QK_ROPE_DIM, cos, ssin)
    k_rope = (kr2[:, :QK_ROPE_DIM] + kr2[:, QK_ROPE_DIM:]).astype(jnp.bfloat16)
    ckv = _rmsnorm(z[:, OFF_CKV:OFF_CKV + KV_LORA_RANK], kvg_ref[...]).astype(jnp.bfloat16)
    cq = _rmsnorm(z[:, OFF_CQ:OFF_CQ + Q_LORA_RANK], qg_ref[...]).astype(jnp.bfloat16)

    k_nope = jnp.dot(ckv, wuk_ref[...], preferred_element_type=jnp.float32).astype(jnp.bfloat16)
    for h in range(N_HEADS):
        k_ref[0, h, :, 0:QK_NOPE_DIM] = k_nope[:, h * QK_NOPE_DIM:(h + 1) * QK_NOPE_DIM]
        k_ref[0, h, :, QK_NOPE_DIM:QK_DIM] = k_rope
    vt_ref[0] = jax.lax.dot_general(wuvt_ref[...], ckv, (((1,), (1,)), ((), ())),
                                    preferred_element_type=jnp.float32).astype(jnp.bfloat16)

    scale = math.log2(math.e) / math.sqrt(QK_DIM)
    qr = jnp.dot(cq, wqr_ref[...], preferred_element_type=jnp.float32)
    n_rep = N_HEADS * QK_ROPE_DIM // LANES
    cos_h = jnp.concatenate([cos] * n_rep, axis=1)
    ssin_h = jnp.concatenate([ssin] * n_rep, axis=1)
    half = QK_ROPE_DIM // 2
    qs = jnp.where(jnp.concatenate([first_half] * n_rep, axis=1),
                   pltpu.roll(qr, qr.shape[1] - half, axis=1), pltpu.roll(qr, half, axis=1))
    q_rope = ((qr * cos_h + qs * ssin_h) * scale).astype(jnp.bfloat16)
    q_nope = (jnp.dot(cq, wq_ref[...], preferred_element_type=jnp.float32) * scale).astype(jnp.bfloat16)
    for h in range(N_HEADS):
        q_ref[0, h, :, 0:QK_NOPE_DIM] = q_nope[:, h * QK_NOPE_DIM:(h + 1) * QK_NOPE_DIM]
        q_ref[0, h, :, QK_NOPE_DIM:QK_DIM] = q_rope[:, h * QK_ROPE_DIM:(h + 1) * QK_ROPE_DIM]


def _mixer_in(x, cos, sin, pre_g, w_in, qg, wq, wqr, kvg, wuk, wuv):
    b, s, d = x.shape
    tm = MIX_IN_TM
    heads = lambda w: pl.BlockSpec((1, N_HEADS, tm, w), lambda bi, i: (bi, 0, i, 0))
    rows = lambda w: pl.BlockSpec((1, tm, w), lambda bi, i: (bi, i, 0))
    return pl.pallas_call(
        _mixer_in_kernel,
        grid=(b, s // tm),
        in_specs=[rows(d), rows(LANES), rows(LANES), _resident((1, d)),
                  pl.BlockSpec((None,) + w_in.shape[1:], lambda *_: (0, 0, 0), pipeline_mode=pl.Buffered(1)),
                  _resident(qg.shape),
                  _resident(wq.shape), _resident(wqr.shape), _resident(kvg.shape),
                  _resident(wuk.shape), _resident(wuv.shape)],
        out_specs=[heads(QK_DIM), heads(QK_DIM), pl.BlockSpec((1, N_HEADS * V_HEAD_DIM, tm), lambda bi, i: (bi, 0, i)), rows(POOL_WIDTH), rows(2 * D_MODEL)],
        out_shape=[jax.ShapeDtypeStruct((b, N_HEADS, s, QK_DIM), jnp.bfloat16),
                   jax.ShapeDtypeStruct((b, N_HEADS, s, QK_DIM), jnp.bfloat16),
                   jax.ShapeDtypeStruct((b, N_HEADS * V_HEAD_DIM, s), jnp.bfloat16),
                   jax.ShapeDtypeStruct((b, s, POOL_WIDTH), jnp.float32),
                   jax.ShapeDtypeStruct((b, s, 2 * D_MODEL), jnp.bfloat16)],
        scratch_shapes=[pltpu.VMEM((d, IN_WIDTH_EXT), jnp.bfloat16)],
        compiler_params=pltpu.CompilerParams(dimension_semantics=("arbitrary", "arbitrary"),
                                             vmem_limit_bytes=VMEM_LIMIT),
        name="mixer_in",
    )(x, cos, sin, pre_g, w_in, qg, wq, wqr, kvg, wuk, wuv)


def _attn_kernel(q_ref, k_ref, vt_ref, o_ref, st_ref, p_ref):
    n_heads, s_len = k_ref.shape[1], k_ref.shape[2]
    n_sub = s_len // ATT_TQ
    n_chunk = s_len // ATT_KC
    tiles = [(hh, j) for hh in range(n_heads) for j in range(n_sub)]

    def logits(t):
        hh, j = tiles[t]
        st = jax.lax.dot_general(k_ref[0, hh], q_ref[0, hh, j * ATT_TQ:(j + 1) * ATT_TQ, :],
                                 (((1,), (1,)), ((), ())), preferred_element_type=jnp.float32)
        st_ref[t % 2] = st
        return jnp.max(st, axis=0, keepdims=True)

    m_next = logits(0)
    for t, (hh, j) in enumerate(tiles):
        slot = t % 2
        m = m_next
        if t + 1 < len(tiles):
            m_next = logits(t + 1)
        l = jnp.zeros_like(m)
        for c in range(n_chunk):
            keys = slice(c * ATT_KC, (c + 1) * ATT_KC)
            p = jnp.exp2(st_ref[slot, keys, :] - m)
            l = l + jnp.sum(p, axis=0, keepdims=True)
            p_ref[slot, keys, :] = p.astype(jnp.bfloat16)
        ot = jnp.dot(vt_ref[0, hh], p_ref[slot], preferred_element_type=jnp.float32)
        o_ref[0, j * ATT_TQ:(j + 1) * ATT_TQ, hh * V_HEAD_DIM:(hh + 1) * V_HEAD_DIM] = (
            (ot / l).T.astype(jnp.bfloat16))


def _attention(q, k, vt):
    b, h, s, _ = q.shape
    hs = ATT_HEADS
    head = lambda r, c: pl.BlockSpec((1, hs, r, c), lambda bi, hi: (bi, hi, 0, 0))
    return pl.pallas_call(
        _attn_kernel,
        grid=(b, h // hs),
        in_specs=[head(s, QK_DIM), head(s, QK_DIM), head(V_HEAD_DIM, s)],
        out_specs=pl.BlockSpec((1, s, hs * V_HEAD_DIM), lambda bi, hi: (bi, 0, hi)),
        out_shape=jax.ShapeDtypeStruct((b, s, h * V_HEAD_DIM), jnp.bfloat16),
        scratch_shapes=[pltpu.VMEM((2, s, ATT_TQ), jnp.float32), pltpu.VMEM((2, s, ATT_TQ), jnp.bfloat16)],
        compiler_params=pltpu.CompilerParams(dimension_semantics=("arbitrary", "arbitrary"),
                                             vmem_limit_bytes=VMEM_LIMIT),
        name="attention",
    )(q, k, vt)


def _mixer_out_kernel(o_ref, pool_ref, prev_ref, next_ref, gate_ref, x_ref, wo_ref, pw_ref, ps_ref, wop_ref,
                      wout_ref, post_g_ref, *rest, seq_len):
    n_w = (len(rest) - 2) // 2
    out_ref, ext_ref = rest[n_w], rest[-1]
    _cast_stream_step(rest[:n_w], rest[n_w + 1:-1])
    i = pl.program_id(1)
    n_i = pl.num_programs(1)
    tm = pool_ref.shape[1]

    ext_ref[0:POOL_HALO, :] = jnp.where(i > 0, prev_ref[0], 0.0)
    ext_ref[POOL_HALO:POOL_HALO + tm, :] = pool_ref[0]
    ext_ref[POOL_HALO + tm:POOL_HALO + tm + POOL_HALO, :] = jnp.where(i < n_i - 1, next_ref[0], 0.0)

    win = MIX_SUB + 2 * POOL_HALO
    for r in range(tm // MIX_SUB):
        r0 = r * MIX_SUB
        rows = slice(r0, r0 + MIX_SUB)
        o_rows = o_ref[0, rows, :]
        t_first = i * tm + r0 + jax.lax.broadcasted_iota(jnp.int32, (POOL_HALO, 1), 0)
        t_last = t_first + (MIX_SUB - POOL_HALO)
        ys, y_attn_parts = [], []
        n_col = D_MODEL // len(POOL_WINDOWS)
        for gi, w in enumerate(POOL_WINDOWS):
            y_attn_parts.append(jnp.dot(o_rows, wo_ref[:, gi * n_col:(gi + 1) * n_col],
                                        preferred_element_type=jnp.float32))
            left = w // 2
            right = w - 1 - left
            cols = slice(gi * POOL_GROUP, (gi + 1) * POOL_GROUP)
            xe = ext_ref[r0:r0 + win, cols]
            fwd, span = xe, 1
            while span < min(w, POOL_HALO):
                fwd = fwd + pltpu.roll(fwd, win - span, axis=0)
                span *= 2
            if w > span:
                fwd = fwd + pltpu.roll(fwd, win - span, axis=0)
            total = pltpu.roll(fwd, left, axis=0)[POOL_HALO:POOL_HALO + MIX_SUB]
            inv = [1.0 / (jnp.minimum(tt + right + 1, seq_len) - jnp.maximum(tt - left, 0)).astype(jnp.float32)
                   for tt in (t_first, t_last)]
            mean = jnp.concatenate([total[:POOL_HALO] * inv[0], total[POOL_HALO:-POOL_HALO] * (1.0 / w),
                                    total[-POOL_HALO:] * inv[1]], axis=0)
            dg = mean - xe[POOL_HALO:POOL_HALO + MIX_SUB]
            ys.append(_bdot(dg, pw_ref[gi]))
        y = jnp.concatenate(ys, axis=1) * ps_ref[...]
        y_pool = _bdot(y, wop_ref[...])
        y_attn = jnp.concatenate(y_attn_parts, axis=1)

        g_attn = gate_ref[0, rows, 0:D_MODEL].astype(jnp.float32)
        g_pool = gate_ref[0, rows, D_MODEL:2 * D_MODEL].astype(jnp.float32)
        mixed = _bdot(g_attn * y_attn + g_pool * y_pool, wout_ref[...])
        out_ref[0, rows, :] = x_ref[0, rows, :] + _rmsnorm(mixed, post_g_ref[...])


def _mixer_out(o, pool, gate, x, wo, pw, ps, wop, wout, post_g, weights):
    b, s, d = x.shape
    tm = MIX_OUT_TM
    nb = tm // POOL_HALO
    last = s // POOL_HALO - 1
    n_i = s // tm
    w_in_specs, w_out_specs, w_shapes = _cast_stream_specs(weights, b * n_i, lambda bi, i: bi * n_i + i)
    rows = lambda w: pl.BlockSpec((1, tm, w), lambda bi, i: (bi, i, 0))
    prev = pl.BlockSpec((1, POOL_HALO, POOL_WIDTH), lambda bi, i: (bi, jnp.maximum(i * nb - 1, 0), 0))
    nxt = pl.BlockSpec((1, POOL_HALO, POOL_WIDTH), lambda bi, i: (bi, jnp.minimum((i + 1) * nb, last), 0))
    return pl.pallas_call(
        functools.partial(_mixer_out_kernel, seq_len=s),
        grid=(b, s // tm),
        in_specs=[rows(d), rows(POOL_WIDTH), prev, nxt, rows(2 * d), rows(d), _resident(wo.shape),
                  _resident(pw.shape), _resident(ps.shape), _resident(wop.shape), _resident(wout.shape),
                  _resident(post_g.shape)] + w_in_specs,
        out_specs=[rows(d)] + w_out_specs,
        out_shape=[jax.ShapeDtypeStruct((b, s, d), jnp.float32)] + w_shapes,
        scratch_shapes=[pltpu.VMEM((tm + 2 * POOL_HALO, POOL_WIDTH), jnp.float32)],
        compiler_params=pltpu.CompilerParams(dimension_semantics=("arbitrary", "arbitrary"),
                                             vmem_limit_bytes=VMEM_LIMIT),
        name="mixer_out",
    )(o, pool, pool, pool, gate, x, wo, pw, ps, wop, wout, post_g, *weights)


def _split_w_uq(w_uq):
    r = w_uq.shape[0]
    w3 = w_uq.reshape(r, N_HEADS, QK_DIM)
    nope = w3[:, :, :QK_NOPE_DIM].reshape(r, N_HEADS * QK_NOPE_DIM)
    return nope, w3[:, :, QK_NOPE_DIM:].reshape(r, N_HEADS * QK_ROPE_DIM)


def kernel(x, positions, ffn1_pre_g, ffn1_w_gate, ffn1_w_up, ffn1_w_down, ffn1_post_g, mix_pre_g, w_in, q_a_norm_g, w_uq, kv_a_norm_g, w_uk, w_uv, w_o_attn, pool_w, pool_scale, w_o_pool, w_out, mix_post_g, ffn2_pre_g, ffn2_w_gate, ffn2_w_up, ffn2_w_down, ffn2_post_g, final_g):
    b, s, d = x.shape
    bf = lambda w: w.astype(jnp.bfloat16)
    row = lambda g: g.reshape(1, -1)
    inv_freq = ROPE_THETA ** (-jnp.arange(0, QK_ROPE_DIM, 2, dtype=jnp.float32) / QK_ROPE_DIM)
    n_freq = QK_ROPE_DIM // 2
    invf = jnp.tile(inv_freq, LANES // n_freq).reshape(1, LANES)
    pos_rep = jnp.repeat(positions.reshape(b * s * n_freq // LANES, LANES // n_freq), n_freq, axis=1)
    assert w_in.shape[0] == 1, "one layer: each FFN's weight casts are hosted by the call that precedes it"
    l = 0
    cos, sin = (t.reshape(b, s, LANES) for t in _rope_tables(pos_rep, invf))

    x1 = _ffn(x.reshape(b * s, d), row(ffn1_pre_g[l]), ffn1_w_gate[l], ffn1_w_up[l], ffn1_w_down[l],
              row(ffn1_post_g[l]), row(final_g[l]), False).reshape(b, s, d)
    wq, wqr = _split_w_uq(bf(w_uq[l]))
    q, k, vt, pool, gate = _mixer_in(
        x1, cos, sin, row(mix_pre_g[l]), w_in, row(q_a_norm_g[l]),
        wq, wqr, row(kv_a_norm_g[l]), bf(w_uk[l]), bf(w_uv[l]).T)
    o = _attention(q, k, vt.reshape(b, N_HEADS, V_HEAD_DIM, s))
    x2, *ffn2_w = _mixer_out(o, pool, gate, x1, bf(w_o_attn[l]), bf(pool_w[l]), row(pool_scale[l]),
                             bf(w_o_pool[l]), bf(w_out[l]), row(mix_post_g[l]),
                             [ffn2_w_gate[l], ffn2_w_up[l], ffn2_w_down[l]])
    return _ffn(x2.reshape(b * s, d), row(ffn2_pre_g[l]), *ffn2_w, row(ffn2_post_g[l]), row(final_g[l]),
                True).reshape(b, s, d)
```

```python
import functools
import math

import jax
import jax.numpy as jnp
from jax.experimental import pallas as pl
from jax.experimental.pallas import tpu as pltpu

D_MODEL = 1024
N_HEADS = 8
QK_NOPE_DIM = 128
QK_ROPE_DIM = 64
QK_DIM = QK_NOPE_DIM + QK_ROPE_DIM
V_HEAD_DIM = 128
Q_LORA_RANK = 384
KV_LORA_RANK = 256
ROPE_THETA = 10000.0
POOL_WINDOWS = (2, 4, 8, 16)
POOL_GROUP = 128
POOL_WIDTH = POOL_GROUP * len(POOL_WINDOWS)
D_FF = 2816
MACARON_WEIGHT = 0.5
NORM_EPS = 1e-6

LANES = 128
SUBLANES = 8
POOL_HALO = 8

OFF_CQ = 0
OFF_CKV = OFF_CQ + Q_LORA_RANK
OFF_KR = OFF_CKV + KV_LORA_RANK
OFF_POOL = OFF_KR + 2 * QK_ROPE_DIM
OFF_GATE = OFF_POOL + POOL_WIDTH
IN_WIDTH_EXT = OFF_GATE + 2 * D_MODEL
W_IN_ROWS = 64
FFN_W_STEPS = 11
ROPE_TABLE_ROWS = 512

FFN_TM = 1024
FFN_SUB = 256
MIX_IN_TM = 512
MIX_OUT_TM = 1024
MIX_SUB = 256
ATT_TQ = 512
ATT_KC = 256
ATT_HEADS = 4
VMEM_LIMIT = 56 * 1024 * 1024

def _resident(shape):
    return pl.BlockSpec(shape, lambda *_: (0,) * len(shape), pipeline_mode=pl.Buffered(1))


def _rmsnorm(x, g):
    ms = jnp.mean(x * x, axis=-1, keepdims=True)
    return x * jax.lax.rsqrt(ms + NORM_EPS) * g


def _bdot(a, b):
    return jnp.dot(a.astype(jnp.bfloat16), b, preferred_element_type=jnp.float32)


def _ffn_kernel(x_ref, pre_g_ref, wg_ref, wu_ref, wd_ref, post_g_ref, final_g_ref, o_ref, *w_scratch, final_norm,
                w_steps):
    weights = w_scratch if w_steps else (wg_ref, wu_ref, wd_ref)

    def row_tile():
        for r in range(x_ref.shape[0] // FFN_SUB):
            rows = slice(r * FFN_SUB, (r + 1) * FFN_SUB)
            x = x_ref[rows, :]
            xn = _rmsnorm(x, pre_g_ref[...]).astype(jnp.bfloat16)
            g = jnp.dot(xn, weights[0][...], preferred_element_type=jnp.float32)
            u = jnp.dot(xn, weights[1][...], preferred_element_type=jnp.float32)
            h = g * (1.0 / (1.0 + jnp.exp(-g))) * u
            f = _bdot(h, weights[2][...])
            y = x + MACARON_WEIGHT * _rmsnorm(f, post_g_ref[...])
            if final_norm:
                y = _rmsnorm(y, final_g_ref[...])
            o_ref[rows, :] = y

    if not w_steps:
        row_tile()
        return
    i = pl.program_id(0)
    fc = wg_ref.shape[1]
    for c in range(w_steps):
        @pl.when(i == c)
        def _(c=c):
            cols = slice(c * fc, (c + 1) * fc)
            weights[0][:, cols] = wg_ref[...].astype(jnp.bfloat16)
            weights[1][:, cols] = wu_ref[...].astype(jnp.bfloat16)
            weights[2][cols, :] = wd_ref[...].astype(jnp.bfloat16)
    pl.when(i >= w_steps)(row_tile)


def _ffn(x2d, pre_g, wg, wu, wd, post_g, final_g, final_norm):
    n, d = x2d.shape
    f = wg.shape[1]
    tm = FFN_TM
    w_steps = 0 if wg.dtype == jnp.bfloat16 else FFN_W_STEPS
    if w_steps:
        fc = f // w_steps
        chunk = lambda i: jnp.minimum(i, w_steps - 1)
        w_specs = [pl.BlockSpec((d, fc), lambda i: (0, chunk(i))), pl.BlockSpec((d, fc), lambda i: (0, chunk(i))),
                   pl.BlockSpec((fc, d), lambda i: (chunk(i), 0))]
        w_scratch = [pltpu.VMEM((d, f), jnp.bfloat16), pltpu.VMEM((d, f), jnp.bfloat16),
                     pltpu.VMEM((f, d), jnp.bfloat16)]
    else:
        w_specs, w_scratch = [_resident(wg.shape), _resident(wu.shape), _resident(wd.shape)], []
    row = pl.BlockSpec((tm, d), lambda i: (jnp.maximum(i - w_steps, 0), 0))
    return pl.pallas_call(
        functools.partial(_ffn_kernel, final_norm=final_norm, w_steps=w_steps),
        grid=(w_steps + n // tm,),
        in_specs=[row, _resident((1, d))] + w_specs + [_resident((1, d)), _resident((1, d))],
        out_specs=row,
        out_shape=jax.ShapeDtypeStruct((n, d), jnp.float32),
        scratch_shapes=w_scratch,
        compiler_params=pltpu.CompilerParams(dimension_semantics=("arbitrary",), vmem_limit_bytes=VMEM_LIMIT),
        name="ffn_final" if final_norm else "ffn",
    )(x2d, pre_g, wg, wu, wd, post_g, final_g)


def _cast_stream_specs(weights, n_steps, step_of):
    spec = lambda w: pl.BlockSpec((w.shape[0] // n_steps, w.shape[1]), lambda *g: (step_of(*g), 0))
    specs = [spec(w) for w in weights]
    return specs, specs, [jax.ShapeDtypeStruct(w.shape, jnp.bfloat16) for w in weights]


def _cast_stream_step(in_refs, out_refs):
    for src, dst in zip(in_refs, out_refs):
        dst[...] = src[...].astype(jnp.bfloat16)


def _rope_table_kernel(pos_ref, invf_ref, cos_ref, sin_ref):
    ang = pos_ref[...].astype(jnp.float32) * invf_ref[...]
    n_freq = QK_ROPE_DIM // 2
    tok_per_row = LANES // n_freq
    rows = ang.shape[0]
    for table, out_ref in ((jnp.cos(ang), cos_ref), (jnp.sin(ang), sin_ref)):
        for k in range(tok_per_row):
            piece = table[:, k * n_freq:(k + 1) * n_freq]
            out_ref[pl.ds(k, rows, stride=tok_per_row), :] = jnp.concatenate([piece] * tok_per_row, axis=1)


def _rope_tables(pos_rep, invf):
    n, _ = pos_rep.shape
    tr = ROPE_TABLE_ROWS
    tok_per_row = LANES // (QK_ROPE_DIM // 2)
    blk = pl.BlockSpec((tr, LANES), lambda i: (i, 0))
    out_blk = pl.BlockSpec((tr * tok_per_row, LANES), lambda i: (i, 0))
    return pl.pallas_call(
        _rope_table_kernel,
        grid=(n // tr,),
        in_specs=[blk, _resident(invf.shape)],
        out_specs=[out_blk, out_blk],
        out_shape=[jax.ShapeDtypeStruct((n * tok_per_row, LANES), jnp.float32)] * 2,
        compiler_params=pltpu.CompilerParams(dimension_semantics=("arbitrary",)),
        name="rope_tables",
    )(pos_rep, invf)


def _extend_w_in(w_ref, wext_ref):
    kr1 = OFF_KR + QK_ROPE_DIM
    half = QK_ROPE_DIM // 2
    n_in = w_ref.shape[1]

    def body(rb, carry):
        rows = pl.ds(pl.multiple_of(rb * W_IN_ROWS, W_IN_ROWS), W_IN_ROWS)
        wext_ref[rows, 0:kr1] = w_ref[rows, 0:kr1].astype(jnp.bfloat16)
        wext_ref[rows, kr1:kr1 + half] = w_ref[rows, OFF_KR + half:kr1].astype(jnp.bfloat16)
        wext_ref[rows, kr1 + half:OFF_POOL] = w_ref[rows, OFF_KR:OFF_KR + half].astype(jnp.bfloat16)
        wext_ref[rows, OFF_POOL:IN_WIDTH_EXT] = w_ref[rows, kr1:n_in].astype(jnp.bfloat16)
        return carry

    jax.lax.fori_loop(0, w_ref.shape[0] // W_IN_ROWS, body, 0)


def _mixer_in_kernel(x_ref, cos_ref, sin_ref, pre_g_ref, w_in_ref, *refs):
    weights, (q_ref, k_ref, vt_ref, pool_ref, gate_ref), wext_ref = refs[:-6], refs[-6:-1], refs[-1]

    @pl.when((pl.program_id(0) == 0) & (pl.program_id(1) == 0))
    def _():
        _extend_w_in(w_in_ref, wext_ref)

    weights = (pre_g_ref, wext_ref) + tuple(weights)
    for r in range(x_ref.shape[1] // MIX_SUB):
        rows = pl.ds(r * MIX_SUB, MIX_SUB)
        _mixer_in_tile(x_ref.at[:, rows, :], cos_ref.at[:, rows, :], sin_ref.at[:, rows, :], *weights,
                       q_ref.at[:, :, rows, :], k_ref.at[:, :, rows, :], vt_ref.at[:, :, rows],
                       pool_ref.at[:, rows, :], gate_ref.at[:, rows, :])


def _mixer_in_tile(x_ref, cos_ref, sin_ref, pre_g_ref, w_in_ref, qg_ref, wq_ref, wqr_ref, kvg_ref, wuk_ref, wuvt_ref,
                   q_ref, k_ref, vt_ref, pool_ref, gate_ref):
    u = _rmsnorm(x_ref[0], pre_g_ref[...]).astype(jnp.bfloat16)
    z = jnp.dot(u, w_in_ref[:, :OFF_GATE], preferred_element_type=jnp.float32)
    pool_ref[0] = z[:, OFF_POOL:OFF_POOL + POOL_WIDTH]

    gl = jnp.dot(u, w_in_ref[:, OFF_GATE:], preferred_element_type=jnp.float32)
    gate_ref[0] = (1.0 / (1.0 + jnp.exp(-gl))).astype(jnp.bfloat16)

    cos = cos_ref[0]
    sin = sin_ref[0]
    lane = jax.lax.broadcasted_iota(jnp.int32, cos.shape, 1)
    first_half = (lane % QK_ROPE_DIM) < (QK_ROPE_DIM // 2)
    ssin = jnp.where(first_half, -sin, sin)

    kr2 = z[:, OFF_KR:OFF_KR + LANES] * jnp.where(lane < QK_ROPE_DIM, cos, ssin)
    k_rope = (kr2[:, :QK_ROPE_DIM] + kr2[:, QK_ROPE_DIM:]).astype(jnp.bfloat16)
    ckv = _rmsnorm(z[:, OFF_CKV:OFF_CKV + KV_LORA_RANK], kvg_ref[...]).astype(jnp.bfloat16)
    cq = _rmsnorm(z[:, OFF_CQ:OFF_CQ + Q_LORA_RANK], qg_ref[...]).astype(jnp.bfloat16)

    k_nope = jnp.dot(ckv, wuk_ref[...], preferred_element_type=jnp.float32).astype(jnp.bfloat16)
    for h in range(N_HEADS):
        k_ref[0, h, :, 0:QK_NOPE_DIM] = k_nope[:, h * QK_NOPE_DIM:(h + 1) * QK_NOPE_DIM]
        k_ref[0, h, :, QK_NOPE_DIM:QK_DIM] = k_rope
    vt_ref[0] = jax.lax.dot_general(wuvt_ref[...], ckv, (((1,), (1,)), ((), ())),
                                    preferred_element_type=jnp.float32).astype(jnp.bfloat16)

    scale = math.log2(math.e) / math.sqrt(QK_DIM)
    qr = jnp.dot(cq, wqr_ref[...], preferred_element_type=jnp.float32)
    n_rep = N_HEADS * QK_ROPE_DIM // LANES
    cos_h = jnp.concatenate([cos] * n_rep, axis=1)
    ssin_h = jnp.concatenate([ssin] * n_rep, axis=1)
    half = QK_ROPE_DIM // 2
    qs = jnp.where(jnp.concatenate([first_half] * n_rep, axis=1),
                   pltpu.roll(qr, qr.shape[1] - half, axis=1), pltpu.roll(qr, half, axis=1))
    q_rope = ((qr * cos_h + qs * ssin_h) * scale).astype(jnp.bfloat16)
    q_nope = (jnp.dot(cq, wq_ref[...], preferred_element_type=jnp.float32) * scale).astype(jnp.bfloat16)
    for h in range(N_HEADS):
        q_ref[0, h, :, 0:QK_NOPE_DIM] = q_nope[:, h * QK_NOPE_DIM:(h + 1) * QK_NOPE_DIM]
        q_ref[0, h, :, QK_NOPE_DIM:QK_DIM] = q_rope[:, h * QK_ROPE_DIM:(h + 1) * QK_ROPE_DIM]


def _mixer_in(x, cos, sin, pre_g, w_in, qg, wq, wqr, kvg, wuk, wuv):
    b, s, d = x.shape
    tm = MIX_IN_TM
    heads = lambda w: pl.BlockSpec((1, N_HEADS, tm, w), lambda bi, i: (bi, 0, i, 0))
    rows = lambda w: pl.BlockSpec((1, tm, w), lambda bi, i: (bi, i, 0))
    return pl.pallas_call(
        _mixer_in_kernel,
        grid=(b, s // tm),
        in_specs=[rows(d), rows(LANES), rows(LANES), _resident((1, d)),
                  pl.BlockSpec((None,) + w_in.shape[1:], lambda *_: (0, 0, 0), pipeline_mode=pl.Buffered(1)),
                  _resident(qg.shape),
                  _resident(wq.shape), _resident(wqr.shape), _resident(kvg.shape),
                  _resident(wuk.shape), _resident(wuv.shape)],
        out_specs=[heads(QK_DIM), heads(QK_DIM), pl.BlockSpec((1, N_HEADS * V_HEAD_DIM, tm), lambda bi, i: (bi, 0, i)), rows(POOL_WIDTH), rows(2 * D_MODEL)],
        out_shape=[jax.ShapeDtypeStruct((b, N_HEADS, s, QK_DIM), jnp.bfloat16),
                   jax.ShapeDtypeStruct((b, N_HEADS, s, QK_DIM), jnp.bfloat16),
                   jax.ShapeDtypeStruct((b, N_HEADS * V_HEAD_DIM, s), jnp.bfloat16),
                   jax.ShapeDtypeStruct((b, s, POOL_WIDTH), jnp.float32),
                   jax.ShapeDtypeStruct((b, s, 2 * D_MODEL), jnp.bfloat16)],
        scratch_shapes=[pltpu.VMEM((d, IN_WIDTH_EXT), jnp.bfloat16)],
        compiler_params=pltpu.CompilerParams(dimension_semantics=("arbitrary", "arbitrary"),
                                             vmem_limit_bytes=VMEM_LIMIT),
        name="mixer_in",
    )(x, cos, sin, pre_g, w_in, qg, wq, wqr, kvg, wuk, wuv)


def _attn_kernel(q_ref, k_ref, vt_ref, o_ref, st_ref, p_ref):
    n_heads, s_len = k_ref.shape[1], k_ref.shape[2]
    n_sub = s_len // ATT_TQ
    n_chunk = s_len // ATT_KC
    tiles = [(hh, j) for hh in range(n_heads) for j in range(n_sub)]

    def logits(t):
        hh, j = tiles[t]
        st = jax.lax.dot_general(k_ref[0, hh], q_ref[0, hh, j * ATT_TQ:(j + 1) * ATT_TQ, :],
                                 (((1,), (1,)), ((), ())), preferred_element_type=jnp.float32)
        st_ref[t % 2] = st
        return jnp.max(st, axis=0, keepdims=True)

    m_next = logits(0)
    for t, (hh, j) in enumerate(tiles):
        slot = t % 2
        m = m_next
        if t + 1 < len(tiles):
            m_next = logits(t + 1)
        l = jnp.zeros_like(m)
        for c in range(n_chunk):
            keys = slice(c * ATT_KC, (c + 1) * ATT_KC)
            p = jnp.exp2(st_ref[slot, keys, :] - m)
            l = l + jnp.sum(p, axis=0, keepdims=True)
            p_ref[slot, keys, :] = p.astype(jnp.bfloat16)
        ot = jnp.dot(vt_ref[0, hh], p_ref[slot], preferred_element_type=jnp.float32)
        o_ref[0, j * ATT_TQ:(j + 1) * ATT_TQ, hh * V_HEAD_DIM:(hh + 1) * V_HEAD_DIM] = (
            (ot / l).T.astype(jnp.bfloat16))


def _attention(q, k, vt):
    b, h, s, _ = q.shape
    hs = ATT_HEADS
    head = lambda r, c: pl.BlockSpec((1, hs, r, c), lambda bi, hi: (bi, hi, 0, 0))
    return pl.pallas_call(
        _attn_kernel,
        grid=(b, h // hs),
        in_specs=[head(s, QK_DIM), head(s, QK_DIM), head(V_HEAD_DIM, s)],
        out_specs=pl.BlockSpec((1, s, hs * V_HEAD_DIM), lambda bi, hi: (bi, 0, hi)),
        out_shape=jax.ShapeDtypeStruct((b, s, h * V_HEAD_DIM), jnp.bfloat16),
        scratch_shapes=[pltpu.VMEM((2, s, ATT_TQ), jnp.float32), pltpu.VMEM((2, s, ATT_TQ), jnp.bfloat16)],
        compiler_params=pltpu.CompilerParams(dimension_semantics=("arbitrary", "arbitrary"),
                                             vmem_limit_bytes=VMEM_LIMIT),
        name="attention",
    )(q, k, vt)


def _mixer_out_kernel(o_ref, pool_ref, prev_ref, next_ref, gate_ref, x_ref, wo_ref, pw_ref, ps_ref, wop_ref,
                      wout_ref, post_g_ref, *rest, seq_len):
    n_w = (len(rest) - 2) // 2
    out_ref, ext_ref = rest[n_w], rest[-1]
    _cast_stream_step(rest[:n_w], rest[n_w + 1:-1])
    i = pl.program_id(1)
    n_i = pl.num_programs(1)
    tm = pool_ref.shape[1]

    ext_ref[0:POOL_HALO, :] = jnp.where(i > 0, prev_ref[0], 0.0)
    ext_ref[POOL_HALO:POOL_HALO + tm, :] = pool_ref[0]
    ext_ref[POOL_HALO + tm:POOL_HALO + tm + POOL_HALO, :] = jnp.where(i < n_i - 1, next_ref[0], 0.0)

    win = MIX_SUB + 2 * POOL_HALO
    for r in range(tm // MIX_SUB):
        r0 = r * MIX_SUB
        rows = slice(r0, r0 + MIX_SUB)
        o_rows = o_ref[0, rows, :]
        t_first = i * tm + r0 + jax.lax.broadcasted_iota(jnp.int32, (POOL_HALO, 1), 0)
        t_last = t_first + (MIX_SUB - POOL_HALO)
        ys, y_attn_parts = [], []
        n_col = D_MODEL // len(POOL_WINDOWS)
        for gi, w in enumerate(POOL_WINDOWS):
            y_attn_parts.append(jnp.dot(o_rows, wo_ref[:, gi * n_col:(gi + 1) * n_col],
                                        preferred_element_type=jnp.float32))
            left = w // 2
            right = w - 1 - left
            cols = slice(gi * POOL_GROUP, (gi + 1) * POOL_GROUP)
            xe = ext_ref[r0:r0 + win, cols]
            fwd, span = xe, 1
            while span < min(w, POOL_HALO):
                fwd = fwd + pltpu.roll(fwd, win - span, axis=0)
                span *= 2
            if w > span:
                fwd = fwd + pltpu.roll(fwd, win - span, axis=0)
            total = pltpu.roll(fwd, left, axis=0)[POOL_HALO:POOL_HALO + MIX_SUB]
            inv = [1.0 / (jnp.minimum(tt + right + 1, seq_len) - jnp.maximum(tt - left, 0)).astype(jnp.float32)
                   for tt in (t_first, t_last)]
            mean = jnp.concatenate([total[:POOL_HALO] * inv[0], total[POOL_HALO:-POOL_HALO] * (1.0 / w),
                                    total[-POOL_HALO:] * inv[1]], axis=0)
            dg = mean - xe[POOL_HALO:POOL_HALO + MIX_SUB]
            ys.append(_bdot(dg, pw_ref[gi]))
        y = jnp.concatenate(ys, axis=1) * ps_ref[...]
        y_pool = _bdot(y, wop_ref[...])
        y_attn = jnp.concatenate(y_attn_parts, axis=1)

        g_attn = gate_ref[0, rows, 0:D_MODEL].astype(jnp.float32)
        g_pool = gate_ref[0, rows, D_MODEL:2 * D_MODEL].astype(jnp.float32)
        mixed = _bdot(g_attn * y_attn + g_pool * y_pool, wout_ref[...])
        out_ref[0, rows, :] = x_ref[0, rows, :] + _rmsnorm(mixed, post_g_ref[...])


def _mixer_out(o, pool, gate, x, wo, pw, ps, wop, wout, post_g, weights):
    b, s, d = x.shape
    tm = MIX_OUT_TM
    nb = tm // POOL_HALO
    last = s // POOL_HALO - 1
    n_i = s // tm
    w_in_specs, w_out_specs, w_shapes = _cast_stream_specs(weights, b * n_i, lambda bi, i: bi * n_i + i)
    rows = lambda w: pl.BlockSpec((1, tm, w), lambda bi, i: (bi, i, 0))
    prev = pl.BlockSpec((1, POOL_HALO, POOL_WIDTH), lambda bi, i: (bi, jnp.maximum(i * nb - 1, 0), 0))
    nxt = pl.BlockSpec((1, POOL_HALO, POOL_WIDTH), lambda bi, i: (bi, jnp.minimum((i + 1) * nb, last), 0))
    return pl.pallas_call(
        functools.partial(_mixer_out_kernel, seq_len=s),
        grid=(b, s // tm),
        in_specs=[rows(d), rows(POOL_WIDTH), prev, nxt, rows(2 * d), rows(d), _resident(wo.shape),
                  _resident(pw.shape), _resident(ps.shape), _resident(wop.shape), _resident(wout.shape),
                  _resident(post_g.shape)] + w_in_specs,
        out_specs=[rows(d)] + w_out_specs,
        out_shape=[jax.ShapeDtypeStruct((b, s, d), jnp.float32)] + w_shapes,
        scratch_shapes=[pltpu.VMEM((tm + 2 * POOL_HALO, POOL_WIDTH), jnp.float32)],
        compiler_params=pltpu.CompilerParams(dimension_semantics=("arbitrary", "arbitrary"),
                                             vmem_limit_bytes=VMEM_LIMIT),
        name="mixer_out",
    )(o, pool, pool, pool, gate, x, wo, pw, ps, wop, wout, post_g, *weights)


def _split_w_uq(w_uq):
    r = w_uq.shape[0]
    w3 = w_uq.reshape(r, N_HEADS, QK_DIM)
    nope = w3[:, :, :QK_NOPE_DIM].reshape(r, N_HEADS * QK_NOPE_DIM)
    return nope, w3[:, :, QK_NOPE_DIM:].reshape(r, N_HEADS * QK_ROPE_DIM)


def kernel(x, positions, ffn1_pre_g, ffn1_w_gate, ffn1_w_up, ffn1_w_down, ffn1_post_g, mix_pre_g, w_in, q_a_norm_g, w_uq, kv_a_norm_g, w_uk, w_uv, w_o_attn, pool_w, pool_scale, w_o_pool, w_out, mix_post_g, ffn2_pre_g, ffn2_w_gate, ffn2_w_up, ffn2_w_down, ffn2_post_g, final_g):
    b, s, d = x.shape
    bf = lambda w: w.astype(jnp.bfloat16)
    row = lambda g: g.reshape(1, -1)
    inv_freq = ROPE_THETA ** (-jnp.arange(0, QK_ROPE_DIM, 2, dtype=jnp.float32) / QK_ROPE_DIM)
    n_freq = QK_ROPE_DIM // 2
    invf = jnp.tile(inv_freq, LANES // n_freq).reshape(1, LANES)
    pos_rep = jnp.repeat(positions.reshape(b * s * n_freq // LANES, LANES // n_freq), n_freq, axis=1)
    assert w_in.shape[0] == 1, "one layer: each FFN's weight casts are hosted by the call that precedes it"
    l = 0
    cos, sin = (t.reshape(b, s, LANES) for t in _rope_tables(pos_rep, invf))

    x1 = _ffn(x.reshape(b * s, d), row(ffn1_pre_g[l]), ffn1_w_gate[l], ffn1_w_up[l], ffn1_w_down[l],
              row(ffn1_post_g[l]), row(final_g[l]), False).reshape(b, s, d)
    wq, wqr = _split_w_uq(bf(w_uq[l]))
    q, k, vt, pool, gate = _mixer_in(
        x1, cos, sin, row(mix_pre_g[l]), w_in, row(q_a_norm_g[l]),
        wq, wqr, row(kv_a_norm_g[l]), bf(w_uk[l]), bf(w_uv[l]).T)
    o = _attention(q, k, vt.reshape(b, N_HEADS, V_HEAD_DIM, s))
    x2, *ffn2_w = _mixer_out(o, pool, gate, x1, bf(w_o_attn[l]), bf(pool_w[l]), row(pool_scale[l]),
                             bf(w_o_pool[l]), bf(w_out[l]), row(mix_post_g[l]),
                             [ffn2_w_gate[l], ffn2_w_up[l], ffn2_w_down[l]])
    return _ffn(x2.reshape(b * s, d), row(ffn2_pre_g[l]), *ffn2_w, row(ffn2_post_g[l]), row(final_g[l]),
                True).reshape(b, s, d)
```

```python
import functools
import math

import jax
import jax.numpy as jnp
from jax.experimental import pallas as pl
from jax.experimental.pallas import tpu as pltpu

D_MODEL = 1024
N_HEADS = 8
QK_NOPE_DIM = 128
QK_ROPE_DIM = 64
QK_DIM = QK_NOPE_DIM + QK_ROPE_DIM
V_HEAD_DIM = 128
Q_LORA_RANK = 384
KV_LORA_RANK = 256
ROPE_THETA = 10000.0
POOL_WINDOWS = (2, 4, 8, 16)
POOL_GROUP = 128
POOL_WIDTH = POOL_GROUP * len(POOL_WINDOWS)
D_FF = 2816
MACARON_WEIGHT = 0.5
NORM_EPS = 1e-6

LANES = 128
POOL_HALO = 8

OFF_CQ = 0
OFF_CKV = OFF_CQ + Q_LORA_RANK
OFF_KR = OFF_CKV + KV_LORA_RANK
OFF_POOL = OFF_KR + 2 * QK_ROPE_DIM
OFF_GATE = OFF_POOL + POOL_WIDTH
IN_WIDTH_EXT = OFF_GATE + 2 * D_MODEL
W_IN_ROWS = 64
FFN_W_STEPS = 11
ROPE_TABLE_ROWS = 512

FFN_TM = 1024
FFN_SUB = 256
MIX_IN_TM = 512
MIX_OUT_TM = 1024
MIX_SUB = 256
ATT_TQ = 512
ATT_KC = 256
ATT_HEADS = 4
VMEM_LIMIT = 56 * 1024 * 1024


def _resident(shape):
    return pl.BlockSpec(shape, lambda *_: (0,) * len(shape), pipeline_mode=pl.Buffered(1))


def _rmsnorm(x, g):
    ms = jnp.mean(x * x, axis=-1, keepdims=True)
    return x * jax.lax.rsqrt(ms + NORM_EPS) * g


def _bdot(a, b):
    return jnp.dot(a.astype(jnp.bfloat16), b, preferred_element_type=jnp.float32)


def _rope_table_block(pos_ref, invf_ref, cos_ref, sin_ref):
    ang = pos_ref[...].astype(jnp.float32) * invf_ref[...]
    n_freq = QK_ROPE_DIM // 2
    tok_per_row = LANES // n_freq
    rows = ang.shape[0]
    for table, out_ref in ((jnp.cos(ang), cos_ref), (jnp.sin(ang), sin_ref)):
        for k in range(tok_per_row):
            piece = table[:, k * n_freq:(k + 1) * n_freq]
            out_ref[pl.ds(k, rows, stride=tok_per_row), :] = jnp.concatenate([piece] * tok_per_row, axis=1)


def _ffn_kernel(x_ref, pre_g_ref, wg_ref, wu_ref, wd_ref, post_g_ref, final_g_ref, *rest, final_norm, w_steps,
                rope_steps):
    if rope_steps:
        pos_ref, invf_ref, o_ref, cos_ref, sin_ref, *w_scratch = rest
    else:
        o_ref, *w_scratch = rest
    weights = w_scratch if w_steps else (wg_ref, wu_ref, wd_ref)

    def row_tile():
        for r in range(x_ref.shape[0] // FFN_SUB):
            rows = slice(r * FFN_SUB, (r + 1) * FFN_SUB)
            x = x_ref[rows, :]
            xn = _rmsnorm(x, pre_g_ref[...]).astype(jnp.bfloat16)
            g = jnp.dot(xn, weights[0][...], preferred_element_type=jnp.float32)
            u = jnp.dot(xn, weights[1][...], preferred_element_type=jnp.float32)
            h = g * (1.0 / (1.0 + jnp.exp(-g))) * u
            f = _bdot(h, weights[2][...])
            y = x + MACARON_WEIGHT * _rmsnorm(f, post_g_ref[...])
            if final_norm:
                y = _rmsnorm(y, final_g_ref[...])
            o_ref[rows, :] = y

    if not w_steps:
        row_tile()
        return
    i = pl.program_id(0)
    fc = wg_ref.shape[1]
    for c in range(w_steps):
        @pl.when(i == c)
        def _(c=c):
            cols = slice(c * fc, (c + 1) * fc)
            weights[0][:, cols] = wg_ref[...].astype(jnp.bfloat16)
            weights[1][:, cols] = wu_ref[...].astype(jnp.bfloat16)
            weights[2][cols, :] = wd_ref[...].astype(jnp.bfloat16)
    if rope_steps:
        pl.when(i < rope_steps)(lambda: _rope_table_block(pos_ref, invf_ref, cos_ref, sin_ref))
    pl.when(i >= w_steps)(row_tile)


def _ffn(x2d, pre_g, wg, wu, wd, post_g, final_g, final_norm, rope=None):
    n, d = x2d.shape
    f = wg.shape[1]
    tm = FFN_TM
    w_steps = 0 if wg.dtype == jnp.bfloat16 else FFN_W_STEPS
    rope_steps, rope_in, rope_specs, rope_out_specs, rope_shapes = 0, [], [], [], []
    if rope is not None:
        pos_rep, invf = rope
        tr = ROPE_TABLE_ROWS
        tok_per_row = LANES // (QK_ROPE_DIM // 2)
        rope_steps = pos_rep.shape[0] // tr
        assert 0 < rope_steps <= w_steps
        blk = lambda i: (jnp.minimum(i, rope_steps - 1), 0)
        rope_in = [pos_rep, invf]
        rope_specs = [pl.BlockSpec((tr, LANES), blk), _resident(invf.shape)]
        rope_out_specs = [pl.BlockSpec((tr * tok_per_row, LANES), blk)] * 2
        rope_shapes = [jax.ShapeDtypeStruct((pos_rep.shape[0] * tok_per_row, LANES), jnp.float32)] * 2
    if w_steps:
        fc = f // w_steps
        chunk = lambda i: jnp.minimum(i, w_steps - 1)
        w_specs = [pl.BlockSpec((d, fc), lambda i: (0, chunk(i))), pl.BlockSpec((d, fc), lambda i: (0, chunk(i))),
                   pl.BlockSpec((fc, d), lambda i: (chunk(i), 0))]
        w_scratch = [pltpu.VMEM((d, f), jnp.bfloat16), pltpu.VMEM((d, f), jnp.bfloat16),
                     pltpu.VMEM((f, d), jnp.bfloat16)]
    else:
        w_specs, w_scratch = [_resident(wg.shape), _resident(wu.shape), _resident(wd.shape)], []
    row = pl.BlockSpec((tm, d), lambda i: (jnp.maximum(i - w_steps, 0), 0))
    outs = pl.pallas_call(
        functools.partial(_ffn_kernel, final_norm=final_norm, w_steps=w_steps, rope_steps=rope_steps),
        grid=(w_steps + n // tm,),
        in_specs=[row, _resident((1, d))] + w_specs + [_resident((1, d)), _resident((1, d))] + rope_specs,
        out_specs=[row] + rope_out_specs,
        out_shape=[jax.ShapeDtypeStruct((n, d), jnp.float32)] + rope_shapes,
        scratch_shapes=w_scratch,
        compiler_params=pltpu.CompilerParams(dimension_semantics=("arbitrary",), vmem_limit_bytes=VMEM_LIMIT),
        name="ffn_final" if final_norm else "ffn",
    )(x2d, pre_g, wg, wu, wd, post_g, final_g, *rope_in)
    return outs if rope_steps else outs[0]


def _cast_stream_specs(weights, n_steps, step_of):
    spec = lambda w: pl.BlockSpec((w.shape[0] // n_steps, w.shape[1]), lambda *g: (step_of(*g), 0))
    specs = [spec(w) for w in weights]
    return specs, specs, [jax.ShapeDtypeStruct(w.shape, jnp.bfloat16) for w in weights]


def _cast_stream_step(in_refs, out_refs):
    for src, dst in zip(in_refs, out_refs):
        dst[...] = src[...].astype(jnp.bfloat16)


def _extend_w_in(w_ref, wext_ref):
    kr1 = OFF_KR + QK_ROPE_DIM
    half = QK_ROPE_DIM // 2
    n_in = w_ref.shape[1]

    def body(rb, carry):
        rows = pl.ds(pl.multiple_of(rb * W_IN_ROWS, W_IN_ROWS), W_IN_ROWS)
        wext_ref[rows, 0:kr1] = w_ref[rows, 0:kr1].astype(jnp.bfloat16)
        wext_ref[rows, kr1:kr1 + half] = w_ref[rows, OFF_KR + half:kr1].astype(jnp.bfloat16)
        wext_ref[rows, kr1 + half:OFF_POOL] = w_ref[rows, OFF_KR:OFF_KR + half].astype(jnp.bfloat16)
        wext_ref[rows, OFF_POOL:IN_WIDTH_EXT] = w_ref[rows, kr1:n_in].astype(jnp.bfloat16)
        return carry

    jax.lax.fori_loop(0, w_ref.shape[0] // W_IN_ROWS, body, 0)


def _mixer_in_kernel(x_ref, cos_ref, sin_ref, pre_g_ref, w_in_ref, *refs):
    weights, (q_ref, k_ref, vt_ref, pool_ref, gate_ref), wext_ref = refs[:-6], refs[-6:-1], refs[-1]

    @pl.when((pl.program_id(0) == 0) & (pl.program_id(1) == 0))
    def _():
        _extend_w_in(w_in_ref, wext_ref)

    weights = (pre_g_ref, wext_ref) + tuple(weights)
    for r in range(x_ref.shape[1] // MIX_SUB):
        rows = pl.ds(r * MIX_SUB, MIX_SUB)
        _mixer_in_tile(x_ref.at[:, rows, :], cos_ref.at[:, rows, :], sin_ref.at[:, rows, :], *weights,
                       q_ref.at[:, :, rows, :], k_ref.at[:, :, rows, :], vt_ref.at[:, :, rows],
                       pool_ref.at[:, rows, :], gate_ref.at[:, rows, :])


def _mixer_in_tile(x_ref, cos_ref, sin_ref, pre_g_ref, w_in_ref, qg_ref, wq_ref, wqr_ref, kvg_ref, wuk_ref, wuvt_ref,
                   q_ref, k_ref, vt_ref, pool_ref, gate_ref):
    u = _rmsnorm(x_ref[0], pre_g_ref[...]).astype(jnp.bfloat16)
    z = jnp.dot(u, w_in_ref[:, :OFF_GATE], preferred_element_type=jnp.float32)
    pool_ref[0] = z[:, OFF_POOL:OFF_POOL + POOL_WIDTH]

    gl = jnp.dot(u, w_in_ref[:, OFF_GATE:], preferred_element_type=jnp.float32)
    gate_ref[0] = (1.0 / (1.0 + jnp.exp(-gl))).astype(jnp.bfloat16)

    cos = cos_ref[0]
    sin = sin_ref[0]
    lane = jax.lax.broadcasted_iota(jnp.int32, cos.shape, 1)
    first_half = (lane % QK_ROPE_DIM) < (QK_ROPE_DIM // 2)
    ssin = jnp.where(first_half, -sin, sin)

    kr2 = z[:, OFF_KR:OFF_KR + LANES] * jnp.where(lane < QK_ROPE_DIM, cos, ssin)
    k_rope = (kr2[:, :QK_ROPE_DIM] + kr2[:, QK_ROPE_DIM:]).astype(jnp.bfloat16)
    ckv = _rmsnorm(z[:, OFF_CKV:OFF_CKV + KV_LORA_RANK], kvg_ref[...]).astype(jnp.bfloat16)
    cq = _rmsnorm(z[:, OFF_CQ:OFF_CQ + Q_LORA_RANK], qg_ref[...]).astype(jnp.bfloat16)

    k_nope = jnp.dot(ckv, wuk_ref[...], preferred_element_type=jnp.float32).astype(jnp.bfloat16)
    for h in range(N_HEADS):
        k_ref[0, h, :, 0:QK_NOPE_DIM] = k_nope[:, h * QK_NOPE_DIM:(h + 1) * QK_NOPE_DIM]
        k_ref[0, h, :, QK_NOPE_DIM:QK_DIM] = k_rope
    vt_ref[0] = jax.lax.dot_general(wuvt_ref[...], ckv, (((1,), (1,)), ((), ())),
                                    preferred_element_type=jnp.float32).astype(jnp.bfloat16)

    scale = math.log2(math.e) / math.sqrt(QK_DIM)
    qr = jnp.dot(cq, wqr_ref[...], preferred_element_type=jnp.float32)
    n_rep = N_HEADS * QK_ROPE_DIM // LANES
    cos_h = jnp.concatenate([cos] * n_rep, axis=1)
    ssin_h = jnp.concatenate([ssin] * n_rep, axis=1)
    half = QK_ROPE_DIM // 2
    qs = jnp.where(jnp.concatenate([first_half] * n_rep, axis=1),
                   pltpu.roll(qr, qr.shape[1] - half, axis=1), pltpu.roll(qr, half, axis=1))
    q_rope = ((qr * cos_h + qs * ssin_h) * scale).astype(jnp.bfloat16)
    q_nope = (jnp.dot(cq, wq_ref[...], preferred_element_type=jnp.float32) * scale).astype(jnp.bfloat16)
    for h in range(N_HEADS):
        q_ref[0, h, :, 0:QK_NOPE_DIM] = q_nope[:, h * QK_NOPE_DIM:(h + 1) * QK_NOPE_DIM]
        q_ref[0, h, :, QK_NOPE_DIM:QK_DIM] = q_rope[:, h * QK_ROPE_DIM:(h + 1) * QK_ROPE_DIM]


def _mixer_in(x, cos, sin, pre_g, w_in, qg, wq, wqr, kvg, wuk, wuv):
    b, s, d = x.shape
    tm = MIX_IN_TM
    heads = lambda w: pl.BlockSpec((1, N_HEADS, tm, w), lambda bi, i: (bi, 0, i, 0))
    rows = lambda w: pl.BlockSpec((1, tm, w), lambda bi, i: (bi, i, 0))
    return pl.pallas_call(
        _mixer_in_kernel,
        grid=(b, s // tm),
        in_specs=[rows(d), rows(LANES), rows(LANES), _resident((1, d)),
                  pl.BlockSpec((None,) + w_in.shape[1:], lambda *_: (0, 0, 0), pipeline_mode=pl.Buffered(1)),
                  _resident(qg.shape),
                  _resident(wq.shape), _resident(wqr.shape), _resident(kvg.shape),
                  _resident(wuk.shape), _resident(wuv.shape)],
        out_specs=[heads(QK_DIM), heads(QK_DIM), pl.BlockSpec((1, N_HEADS * V_HEAD_DIM, tm), lambda bi, i: (bi, 0, i)), rows(POOL_WIDTH), rows(2 * D_MODEL)],
        out_shape=[jax.ShapeDtypeStruct((b, N_HEADS, s, QK_DIM), jnp.bfloat16),
                   jax.ShapeDtypeStruct((b, N_HEADS, s, QK_DIM), jnp.bfloat16),
                   jax.ShapeDtypeStruct((b, N_HEADS * V_HEAD_DIM, s), jnp.bfloat16),
                   jax.ShapeDtypeStruct((b, s, POOL_WIDTH), jnp.float32),
                   jax.ShapeDtypeStruct((b, s, 2 * D_MODEL), jnp.bfloat16)],
        scratch_shapes=[pltpu.VMEM((d, IN_WIDTH_EXT), jnp.bfloat16)],
        compiler_params=pltpu.CompilerParams(dimension_semantics=("arbitrary", "arbitrary"),
                                             vmem_limit_bytes=VMEM_LIMIT),
        name="mixer_in",
    )(x, cos, sin, pre_g, w_in, qg, wq, wqr, kvg, wuk, wuv)


def _attn_kernel(q_ref, k_ref, vt_ref, o_ref, st_ref, p_ref):
    n_heads, s_len = k_ref.shape[1], k_ref.shape[2]
    n_sub = s_len // ATT_TQ
    n_chunk = s_len // ATT_KC
    tiles = [(hh, j) for hh in range(n_heads) for j in range(n_sub)]

    def logits(t):
        hh, j = tiles[t]
        st = jax.lax.dot_general(k_ref[0, hh], q_ref[0, hh, j * ATT_TQ:(j + 1) * ATT_TQ, :],
                                 (((1,), (1,)), ((), ())), preferred_element_type=jnp.float32)
        st_ref[t % 2] = st
        return jnp.max(st, axis=0, keepdims=True)

    m_next = logits(0)
    for t, (hh, j) in enumerate(tiles):
        slot = t % 2
        m = m_next
        if t + 1 < len(tiles):
            m_next = logits(t + 1)
        l = jnp.zeros_like(m)
        for c in range(n_chunk):
            keys = slice(c * ATT_KC, (c + 1) * ATT_KC)
            p = jnp.exp2(st_ref[slot, keys, :] - m)
            l = l + jnp.sum(p, axis=0, keepdims=True)
            p_ref[slot, keys, :] = p.astype(jnp.bfloat16)
        ot = jnp.dot(vt_ref[0, hh], p_ref[slot], preferred_element_type=jnp.float32)
        o_ref[0, j * ATT_TQ:(j + 1) * ATT_TQ, hh * V_HEAD_DIM:(hh + 1) * V_HEAD_DIM] = (
            (ot / l).T.astype(jnp.bfloat16))


def _attention(q, k, vt):
    b, h, s, _ = q.shape
    hs = ATT_HEADS
    head = lambda r, c: pl.BlockSpec((1, hs, r, c), lambda bi, hi: (bi, hi, 0, 0))
    return pl.pallas_call(
        _attn_kernel,
        grid=(b, h // hs),
        in_specs=[head(s, QK_DIM), head(s, QK_DIM), head(V_HEAD_DIM, s)],
        out_specs=pl.BlockSpec((1, s, hs * V_HEAD_DIM), lambda bi, hi: (bi, 0, hi)),
        out_shape=jax.ShapeDtypeStruct((b, s, h * V_HEAD_DIM), jnp.bfloat16),
        scratch_shapes=[pltpu.VMEM((2, s, ATT_TQ), jnp.float32), pltpu.VMEM((2, s, ATT_TQ), jnp.bfloat16)],
        compiler_params=pltpu.CompilerParams(dimension_semantics=("arbitrary", "arbitrary"),
                                             vmem_limit_bytes=VMEM_LIMIT),
        name="attention",
    )(q, k, vt)


def _mixer_out_kernel(o_ref, pool_ref, prev_ref, next_ref, gate_ref, x_ref, wo_ref, pw_ref, ps_ref, wop_ref,
                      wout_ref, post_g_ref, *rest, seq_len):
    n_w = (len(rest) - 2) // 2
    out_ref, ext_ref = rest[n_w], rest[-1]
    _cast_stream_step(rest[:n_w], rest[n_w + 1:-1])
    i = pl.program_id(1)
    n_i = pl.num_programs(1)
    tm = pool_ref.shape[1]

    ext_ref[0:POOL_HALO, :] = jnp.where(i > 0, prev_ref[0], 0.0)
    ext_ref[POOL_HALO:POOL_HALO + tm, :] = pool_ref[0]
    ext_ref[POOL_HALO + tm:POOL_HALO + tm + POOL_HALO, :] = jnp.where(i < n_i - 1, next_ref[0], 0.0)

    win = MIX_SUB + 2 * POOL_HALO
    for r in range(tm // MIX_SUB):
        r0 = r * MIX_SUB
        rows = slice(r0, r0 + MIX_SUB)
        o_rows = o_ref[0, rows, :]
        t_first = i * tm + r0 + jax.lax.broadcasted_iota(jnp.int32, (POOL_HALO, 1), 0)
        t_last = t_first + (MIX_SUB - POOL_HALO)
        ys, y_attn_parts = [], []
        n_col = D_MODEL // len(POOL_WINDOWS)
        for gi, w in enumerate(POOL_WINDOWS):
            y_attn_parts.append(jnp.dot(o_rows, wo_ref[:, gi * n_col:(gi + 1) * n_col],
                                        preferred_element_type=jnp.float32))
            left = w // 2
            right = w - 1 - left
            cols = slice(gi * POOL_GROUP, (gi + 1) * POOL_GROUP)
            xe = ext_ref[r0:r0 + win, cols]
            fwd, span = xe, 1
            while span < min(w, POOL_HALO):
                fwd = fwd + pltpu.roll(fwd, win - span, axis=0)
                span *= 2
            if w > span:
                fwd = fwd + pltpu.roll(fwd, win - span, axis=0)
            total = pltpu.roll(fwd, left, axis=0)[POOL_HALO:POOL_HALO + MIX_SUB]
            inv = [1.0 / (jnp.minimum(tt + right + 1, seq_len) - jnp.maximum(tt - left, 0)).astype(jnp.float32)
                   for tt in (t_first, t_last)]
            mean = jnp.concatenate([total[:POOL_HALO] * inv[0], total[POOL_HALO:-POOL_HALO] * (1.0 / w),
                                    total[-POOL_HALO:] * inv[1]], axis=0)
            dg = mean - xe[POOL_HALO:POOL_HALO + MIX_SUB]
            ys.append(_bdot(dg, pw_ref[gi]))
        y = jnp.concatenate(ys, axis=1) * ps_ref[...]
        y_pool = _bdot(y, wop_ref[...])
        y_attn = jnp.concatenate(y_attn_parts, axis=1)

        g_attn = gate_ref[0, rows, 0:D_MODEL].astype(jnp.float32)
        g_pool = gate_ref[0, rows, D_MODEL:2 * D_MODEL].astype(jnp.float32)
        mixed = _bdot(g_attn * y_attn + g_pool * y_pool, wout_ref[...])
        out_ref[0, rows, :] = x_ref[0, rows, :] + _rmsnorm(mixed, post_g_ref[...])


def _mixer_out(o, pool, gate, x, wo, pw, ps, wop, wout, post_g, weights):
    b, s, d = x.shape
    tm = MIX_OUT_TM
    nb = tm // POOL_HALO
    last = s // POOL_HALO - 1
    n_i = s // tm
    w_in_specs, w_out_specs, w_shapes = _cast_stream_specs(weights, b * n_i, lambda bi, i: bi * n_i + i)
    rows = lambda w: pl.BlockSpec((1, tm, w), lambda bi, i: (bi, i, 0))
    prev = pl.BlockSpec((1, POOL_HALO, POOL_WIDTH), lambda bi, i: (bi, jnp.maximum(i * nb - 1, 0), 0))
    nxt = pl.BlockSpec((1, POOL_HALO, POOL_WIDTH), lambda bi, i: (bi, jnp.minimum((i + 1) * nb, last), 0))
    return pl.pallas_call(
        functools.partial(_mixer_out_kernel, seq_len=s),
        grid=(b, s // tm),
        in_specs=[rows(d), rows(POOL_WIDTH), prev, nxt, rows(2 * d), rows(d), _resident(wo.shape),
                  _resident(pw.shape), _resident(ps.shape), _resident(wop.shape), _resident(wout.shape),
                  _resident(post_g.shape)] + w_in_specs,
        out_specs=[rows(d)] + w_out_specs,
        out_shape=[jax.ShapeDtypeStruct((b, s, d), jnp.float32)] + w_shapes,
        scratch_shapes=[pltpu.VMEM((tm + 2 * POOL_HALO, POOL_WIDTH), jnp.float32)],
        compiler_params=pltpu.CompilerParams(dimension_semantics=("arbitrary", "arbitrary"),
                                             vmem_limit_bytes=VMEM_LIMIT),
        name="mixer_out",
    )(o, pool, pool, pool, gate, x, wo, pw, ps, wop, wout, post_g, *weights)


def _split_w_uq(w_uq):
    r = w_uq.shape[0]
    w3 = w_uq.reshape(r, N_HEADS, QK_DIM)
    nope = w3[:, :, :QK_NOPE_DIM].reshape(r, N_HEADS * QK_NOPE_DIM)
    return nope, w3[:, :, QK_NOPE_DIM:].reshape(r, N_HEADS * QK_ROPE_DIM)


def kernel(x, positions, ffn1_pre_g, ffn1_w_gate, ffn1_w_up, ffn1_w_down, ffn1_post_g, mix_pre_g, w_in, q_a_norm_g, w_uq, kv_a_norm_g, w_uk, w_uv, w_o_attn, pool_w, pool_scale, w_o_pool, w_out, mix_post_g, ffn2_pre_g, ffn2_w_gate, ffn2_w_up, ffn2_w_down, ffn2_post_g, final_g):
    b, s, d = x.shape
    bf = lambda w: w.astype(jnp.bfloat16)
    row = lambda g: g.reshape(1, -1)
    inv_freq = ROPE_THETA ** (-jnp.arange(0, QK_ROPE_DIM, 2, dtype=jnp.float32) / QK_ROPE_DIM)
    n_freq = QK_ROPE_DIM // 2
    invf = jnp.tile(inv_freq, LANES // n_freq).reshape(1, LANES)
    pos_rep = jnp.repeat(positions.reshape(b * s * n_freq // LANES, LANES // n_freq), n_freq, axis=1)
    assert w_in.shape[0] == 1, "one layer: the second FFN's weight casts are hosted by the mixer_out before it"
    l = 0

    x1, cos, sin = _ffn(x.reshape(b * s, d), row(ffn1_pre_g[l]), ffn1_w_gate[l], ffn1_w_up[l], ffn1_w_down[l],
                        row(ffn1_post_g[l]), row(final_g[l]), False, rope=(pos_rep, invf))
    x1, cos, sin = x1.reshape(b, s, d), cos.reshape(b, s, LANES), sin.reshape(b, s, LANES)
    wq, wqr = _split_w_uq(bf(w_uq[l]))
    q, k, vt, pool, gate = _mixer_in(
        x1, cos, sin, row(mix_pre_g[l]), w_in, row(q_a_norm_g[l]),
        wq, wqr, row(kv_a_norm_g[l]), bf(w_uk[l]), bf(w_uv[l]).T)
    o = _attention(q, k, vt.reshape(b, N_HEADS, V_HEAD_DIM, s))
    x2, *ffn2_w = _mixer_out(o, pool, gate, x1, bf(w_o_attn[l]), bf(pool_w[l]), row(pool_scale[l]),
                             bf(w_o_pool[l]), bf(w_out[l]), row(mix_post_g[l]),
                             [ffn2_w_gate[l], ffn2_w_up[l], ffn2_w_down[l]])
    return _ffn(x2.reshape(b * s, d), row(ffn2_pre_g[l]), *ffn2_w, row(ffn2_post_g[l]), row(final_g[l]),
                True).reshape(b, s, d)
```

```python
import functools
import math

import jax
import jax.numpy as jnp
from jax.experimental import pallas as pl
from jax.experimental.pallas import tpu as pltpu

D_MODEL = 1024
N_HEADS = 8
QK_NOPE_DIM = 128
QK_ROPE_DIM = 64
QK_DIM = QK_NOPE_DIM + QK_ROPE_DIM
V_HEAD_DIM = 128
Q_LORA_RANK = 384
KV_LORA_RANK = 256
ROPE_THETA = 10000.0
POOL_WINDOWS = (2, 4, 8, 16)
POOL_GROUP = 128
POOL_WIDTH = POOL_GROUP * len(POOL_WINDOWS)
D_FF = 2816
MACARON_WEIGHT = 0.5
NORM_EPS = 1e-6

LANES = 128
POOL_HALO = 8

OFF_CQ = 0
OFF_CKV = OFF_CQ + Q_LORA_RANK
OFF_KR = OFF_CKV + KV_LORA_RANK
OFF_POOL = OFF_KR + 2 * QK_ROPE_DIM
OFF_GATE = OFF_POOL + POOL_WIDTH
IN_WIDTH_EXT = OFF_GATE + 2 * D_MODEL
FFN_W_STEPS = 11
ROPE_TABLE_ROWS = 512

FFN_TM = 1024
FFN_SUB = 256
MIX_IN_TM = 1024
MIX_OUT_TM = 1024
MIX_SUB = 256
ATT_TQ = 512
ATT_KC = 256
ATT_HEADS = 4
VMEM_LIMIT = 56 * 1024 * 1024


def _resident(shape):
    return pl.BlockSpec(shape, lambda *_: (0,) * len(shape), pipeline_mode=pl.Buffered(1))


def _rmsnorm(x, g):
    ms = jnp.mean(x * x, axis=-1, keepdims=True)
    return x * jax.lax.rsqrt(ms + NORM_EPS) * g


def _bdot(a, b):
    return jnp.dot(a.astype(jnp.bfloat16), b, preferred_element_type=jnp.float32)


def _rope_table_block(pos_ref, invf_ref, cos_ref, sin_ref):
    ang = pos_ref[...].astype(jnp.float32) * invf_ref[...]
    n_freq = QK_ROPE_DIM // 2
    tok_per_row = LANES // n_freq
    rows = ang.shape[0]
    for table, out_ref in ((jnp.cos(ang), cos_ref), (jnp.sin(ang), sin_ref)):
        for k in range(tok_per_row):
            piece = table[:, k * n_freq:(k + 1) * n_freq]
            out_ref[pl.ds(k, rows, stride=tok_per_row), :] = jnp.concatenate([piece] * tok_per_row, axis=1)


def _extend_w_in(w_ref, wext_ref):
    kr1 = OFF_KR + QK_ROPE_DIM
    half = QK_ROPE_DIM // 2
    wext_ref[:, 0:kr1] = w_ref[:, 0:kr1].astype(jnp.bfloat16)
    wext_ref[:, kr1:kr1 + half] = w_ref[:, OFF_KR + half:kr1].astype(jnp.bfloat16)
    wext_ref[:, kr1 + half:OFF_POOL] = w_ref[:, OFF_KR:OFF_KR + half].astype(jnp.bfloat16)
    wext_ref[:, OFF_POOL:IN_WIDTH_EXT] = w_ref[:, kr1:w_ref.shape[1]].astype(jnp.bfloat16)


def _ffn_kernel(x_ref, pre_g_ref, wg_ref, wu_ref, wd_ref, post_g_ref, final_g_ref, *rest, final_norm, w_steps,
                rope_steps, host_w_in):
    rest = list(rest)
    pos_ref, invf_ref = (rest.pop(0), rest.pop(0)) if rope_steps else (None, None)
    w_in_ref = rest.pop(0) if host_w_in else None
    o_ref = rest.pop(0)
    cos_ref, sin_ref = (rest.pop(0), rest.pop(0)) if rope_steps else (None, None)
    wext_ref = rest.pop(0) if host_w_in else None
    w_scratch = rest
    weights = w_scratch if w_steps else (wg_ref, wu_ref, wd_ref)

    def row_tile():
        if host_w_in:
            _extend_w_in(w_in_ref, wext_ref)
        for r in range(x_ref.shape[0] // FFN_SUB):
            rows = slice(r * FFN_SUB, (r + 1) * FFN_SUB)
            x = x_ref[rows, :]
            xn = _rmsnorm(x, pre_g_ref[...]).astype(jnp.bfloat16)
            g = jnp.dot(xn, weights[0][...], preferred_element_type=jnp.float32)
            u = jnp.dot(xn, weights[1][...], preferred_element_type=jnp.float32)
            h = g * (1.0 / (1.0 + jnp.exp(-g))) * u
            f = _bdot(h, weights[2][...])
            y = x + MACARON_WEIGHT * _rmsnorm(f, post_g_ref[...])
            if final_norm:
                y = _rmsnorm(y, final_g_ref[...])
            o_ref[rows, :] = y

    if not w_steps:
        row_tile()
        return
    i = pl.program_id(0)
    fc = wg_ref.shape[1]
    for c in range(w_steps):
        @pl.when(i == c)
        def _(c=c):
            cols = slice(c * fc, (c + 1) * fc)
            weights[0][:, cols] = wg_ref[...].astype(jnp.bfloat16)
            weights[1][:, cols] = wu_ref[...].astype(jnp.bfloat16)
            weights[2][cols, :] = wd_ref[...].astype(jnp.bfloat16)
    if rope_steps:
        pl.when(i < rope_steps)(lambda: _rope_table_block(pos_ref, invf_ref, cos_ref, sin_ref))
    pl.when(i >= w_steps)(row_tile)


def _ffn(x2d, pre_g, wg, wu, wd, post_g, final_g, final_norm, rope=None, w_in=None):
    n, d = x2d.shape
    f = wg.shape[1]
    tm = FFN_TM
    w_steps = 0 if wg.dtype == jnp.bfloat16 else FFN_W_STEPS
    rope_steps, rope_in, rope_specs, rope_out_specs, rope_shapes = 0, [], [], [], []
    if rope is not None:
        pos_rep, invf = rope
        tr = ROPE_TABLE_ROWS
        tok_per_row = LANES // (QK_ROPE_DIM // 2)
        rope_steps = pos_rep.shape[0] // tr
        assert 0 < rope_steps <= w_steps
        blk = lambda i: (jnp.minimum(i, rope_steps - 1), 0)
        rope_in = [pos_rep, invf]
        rope_specs = [pl.BlockSpec((tr, LANES), blk), _resident(invf.shape)]
        rope_out_specs = [pl.BlockSpec((tr * tok_per_row, LANES), blk)] * 2
        rope_shapes = [jax.ShapeDtypeStruct((pos_rep.shape[0] * tok_per_row, LANES), jnp.float32)] * 2
    if w_steps:
        fc = f // w_steps
        chunk = lambda i: jnp.minimum(i, w_steps - 1)
        w_specs = [pl.BlockSpec((d, fc), lambda i: (0, chunk(i))), pl.BlockSpec((d, fc), lambda i: (0, chunk(i))),
                   pl.BlockSpec((fc, d), lambda i: (chunk(i), 0))]
        w_scratch = [pltpu.VMEM((d, f), jnp.bfloat16), pltpu.VMEM((d, f), jnp.bfloat16),
                     pltpu.VMEM((f, d), jnp.bfloat16)]
    else:
        w_specs, w_scratch = [_resident(wg.shape), _resident(wu.shape), _resident(wd.shape)], []
    tile = lambda i: (jnp.maximum(i - w_steps, 0), 0)
    row = pl.BlockSpec((tm, d), tile)
    host_in, host_specs, host_out_specs, host_shapes = [], [], [], []
    if w_in is not None:
        wr = w_in.shape[0] // (n // tm)
        host_in = [w_in]
        host_specs = [pl.BlockSpec((wr, w_in.shape[1]), tile)]
        host_out_specs = [pl.BlockSpec((wr, IN_WIDTH_EXT), tile)]
        host_shapes = [jax.ShapeDtypeStruct((w_in.shape[0], IN_WIDTH_EXT), jnp.bfloat16)]
    outs = pl.pallas_call(
        functools.partial(_ffn_kernel, final_norm=final_norm, w_steps=w_steps, rope_steps=rope_steps,
                          host_w_in=w_in is not None),
        grid=(w_steps + n // tm,),
        in_specs=([row, _resident((1, d))] + w_specs + [_resident((1, d)), _resident((1, d))] + rope_specs
                  + host_specs),
        out_specs=[row] + rope_out_specs + host_out_specs,
        out_shape=[jax.ShapeDtypeStruct((n, d), jnp.float32)] + rope_shapes + host_shapes,
        scratch_shapes=w_scratch,
        compiler_params=pltpu.CompilerParams(dimension_semantics=("arbitrary",), vmem_limit_bytes=VMEM_LIMIT),
        name="ffn_final" if final_norm else "ffn",
    )(x2d, pre_g, wg, wu, wd, post_g, final_g, *rope_in, *host_in)
    return outs if len(outs) > 1 else outs[0]


def _cast_stream_specs(weights, n_steps, step_of):
    spec = lambda w: pl.BlockSpec((w.shape[0] // n_steps, w.shape[1]), lambda *g: (step_of(*g), 0))
    specs = [spec(w) for w in weights]
    return specs, specs, [jax.ShapeDtypeStruct(w.shape, jnp.bfloat16) for w in weights]


def _cast_stream_step(in_refs, out_refs):
    for src, dst in zip(in_refs, out_refs):
        dst[...] = src[...].astype(jnp.bfloat16)


def _mixer_in_kernel(x_ref, cos_ref, sin_ref, *refs):
    weights, (q_ref, k_ref, vt_ref, pool_ref, gate_ref) = refs[:-5], refs[-5:]
    for r in range(x_ref.shape[1] // MIX_SUB):
        rows = pl.ds(r * MIX_SUB, MIX_SUB)
        _mixer_in_tile(x_ref.at[:, rows, :], cos_ref.at[:, rows, :], sin_ref.at[:, rows, :], *weights,
                       q_ref.at[:, :, rows, :], k_ref.at[:, :, rows, :], vt_ref.at[:, :, rows],
                       pool_ref.at[:, rows, :], gate_ref.at[:, rows, :])


def _mixer_in_tile(x_ref, cos_ref, sin_ref, pre_g_ref, w_in_ref, qg_ref, wq_ref, wqr_ref, kvg_ref, wuk_ref, wuvt_ref,
                   q_ref, k_ref, vt_ref, pool_ref, gate_ref):
    u = _rmsnorm(x_ref[0], pre_g_ref[...]).astype(jnp.bfloat16)
    z = jnp.dot(u, w_in_ref[:, :OFF_GATE], preferred_element_type=jnp.float32)
    pool_ref[0] = z[:, OFF_POOL:OFF_POOL + POOL_WIDTH]

    gl = jnp.dot(u, w_in_ref[:, OFF_GATE:], preferred_element_type=jnp.float32)
    gate_ref[0] = (1.0 / (1.0 + jnp.exp(-gl))).astype(jnp.bfloat16)

    cos = cos_ref[0]
    sin = sin_ref[0]
    lane = jax.lax.broadcasted_iota(jnp.int32, cos.shape, 1)
    first_half = (lane % QK_ROPE_DIM) < (QK_ROPE_DIM // 2)
    ssin = jnp.where(first_half, -sin, sin)

    kr2 = z[:, OFF_KR:OFF_KR + LANES] * jnp.where(lane < QK_ROPE_DIM, cos, ssin)
    k_rope = (kr2[:, :QK_ROPE_DIM] + kr2[:, QK_ROPE_DIM:]).astype(jnp.bfloat16)
    ckv = _rmsnorm(z[:, OFF_CKV:OFF_CKV + KV_LORA_RANK], kvg_ref[...]).astype(jnp.bfloat16)
    cq = _rmsnorm(z[:, OFF_CQ:OFF_CQ + Q_LORA_RANK], qg_ref[...]).astype(jnp.bfloat16)

    k_nope = jnp.dot(ckv, wuk_ref[...], preferred_element_type=jnp.float32).astype(jnp.bfloat16)
    for h in range(N_HEADS):
        k_ref[0, h, :, 0:QK_NOPE_DIM] = k_nope[:, h * QK_NOPE_DIM:(h + 1) * QK_NOPE_DIM]
        k_ref[0, h, :, QK_NOPE_DIM:QK_DIM] = k_rope
    vt_ref[0] = jax.lax.dot_general(wuvt_ref[...], ckv, (((1,), (1,)), ((), ())),
                                    preferred_element_type=jnp.float32).astype(jnp.bfloat16)

    scale = math.log2(math.e) / math.sqrt(QK_DIM)
    qr = jnp.dot(cq, wqr_ref[...], preferred_element_type=jnp.float32)
    n_rep = N_HEADS * QK_ROPE_DIM // LANES
    cos_h = jnp.concatenate([cos] * n_rep, axis=1)
    ssin_h = jnp.concatenate([ssin] * n_rep, axis=1)
    half = QK_ROPE_DIM // 2
    qs = jnp.where(jnp.concatenate([first_half] * n_rep, axis=1),
                   pltpu.roll(qr, qr.shape[1] - half, axis=1), pltpu.roll(qr, half, axis=1))
    q_rope = ((qr * cos_h + qs * ssin_h) * scale).astype(jnp.bfloat16)
    q_nope = (jnp.dot(cq, wq_ref[...], preferred_element_type=jnp.float32) * scale).astype(jnp.bfloat16)
    for h in range(N_HEADS):
        q_ref[0, h, :, 0:QK_NOPE_DIM] = q_nope[:, h * QK_NOPE_DIM:(h + 1) * QK_NOPE_DIM]
        q_ref[0, h, :, QK_NOPE_DIM:QK_DIM] = q_rope[:, h * QK_ROPE_DIM:(h + 1) * QK_ROPE_DIM]


def _mixer_in(x, cos, sin, pre_g, w_in, qg, wq, wqr, kvg, wuk, wuv):
    b, s, d = x.shape
    tm = MIX_IN_TM
    heads = lambda w: pl.BlockSpec((1, N_HEADS, tm, w), lambda bi, i: (bi, 0, i, 0))
    rows = lambda w: pl.BlockSpec((1, tm, w), lambda bi, i: (bi, i, 0))
    return pl.pallas_call(
        _mixer_in_kernel,
        grid=(b, s // tm),
        in_specs=[rows(d), rows(LANES), rows(LANES), _resident((1, d)), _resident(w_in.shape), _resident(qg.shape),
                  _resident(wq.shape), _resident(wqr.shape), _resident(kvg.shape),
                  _resident(wuk.shape), _resident(wuv.shape)],
        out_specs=[heads(QK_DIM), heads(QK_DIM), pl.BlockSpec((1, N_HEADS * V_HEAD_DIM, tm), lambda bi, i: (bi, 0, i)), rows(POOL_WIDTH), rows(2 * D_MODEL)],
        out_shape=[jax.ShapeDtypeStruct((b, N_HEADS, s, QK_DIM), jnp.bfloat16),
                   jax.ShapeDtypeStruct((b, N_HEADS, s, QK_DIM), jnp.bfloat16),
                   jax.ShapeDtypeStruct((b, N_HEADS * V_HEAD_DIM, s), jnp.bfloat16),
                   jax.ShapeDtypeStruct((b, s, POOL_WIDTH), jnp.float32),
                   jax.ShapeDtypeStruct((b, s, 2 * D_MODEL), jnp.bfloat16)],
        compiler_params=pltpu.CompilerParams(dimension_semantics=("arbitrary", "arbitrary"),
                                             vmem_limit_bytes=VMEM_LIMIT),
        name="mixer_in",
    )(x, cos, sin, pre_g, w_in, qg, wq, wqr, kvg, wuk, wuv)


def _attn_kernel(q_ref, k_ref, vt_ref, o_ref, st_ref, p_ref):
    n_heads, s_len = k_ref.shape[1], k_ref.shape[2]
    n_sub = s_len // ATT_TQ
    n_chunk = s_len // ATT_KC
    tiles = [(hh, j) for hh in range(n_heads) for j in range(n_sub)]

    def logits(t):
        hh, j = tiles[t]
        st = jax.lax.dot_general(k_ref[0, hh], q_ref[0, hh, j * ATT_TQ:(j + 1) * ATT_TQ, :],
                                 (((1,), (1,)), ((), ())), preferred_element_type=jnp.float32)
        st_ref[t % 2] = st
        return jnp.max(st, axis=0, keepdims=True)

    m_next = logits(0)
    for t, (hh, j) in enumerate(tiles):
        slot = t % 2
        m = m_next
        if t + 1 < len(tiles):
            m_next = logits(t + 1)
        l = jnp.zeros_like(m)
        for c in range(n_chunk):
            keys = slice(c * ATT_KC, (c + 1) * ATT_KC)
            p = jnp.exp2(st_ref[slot, keys, :] - m)
            l = l + jnp.sum(p, axis=0, keepdims=True)
            p_ref[slot, keys, :] = p.astype(jnp.bfloat16)
        ot = jnp.dot(vt_ref[0, hh], p_ref[slot], preferred_element_type=jnp.float32)
        o_ref[0, j * ATT_TQ:(j + 1) * ATT_TQ, hh * V_HEAD_DIM:(hh + 1) * V_HEAD_DIM] = (
            (ot / l).T.astype(jnp.bfloat16))


def _attention(q, k, vt):
    b, h, s, _ = q.shape
    hs = ATT_HEADS
    head = lambda r, c: pl.BlockSpec((1, hs, r, c), lambda bi, hi: (bi, hi, 0, 0))
    return pl.pallas_call(
        _attn_kernel,
        grid=(b, h // hs),
        in_specs=[head(s, QK_DIM), head(s, QK_DIM), head(V_HEAD_DIM, s)],
        out_specs=pl.BlockSpec((1, s, hs * V_HEAD_DIM), lambda bi, hi: (bi, 0, hi)),
        out_shape=jax.ShapeDtypeStruct((b, s, h * V_HEAD_DIM), jnp.bfloat16),
        scratch_shapes=[pltpu.VMEM((2, s, ATT_TQ), jnp.float32), pltpu.VMEM((2, s, ATT_TQ), jnp.bfloat16)],
        compiler_params=pltpu.CompilerParams(dimension_semantics=("arbitrary", "arbitrary"),
                                             vmem_limit_bytes=VMEM_LIMIT),
        name="attention",
    )(q, k, vt)


def _mixer_out_kernel(o_ref, pool_ref, prev_ref, next_ref, gate_ref, x_ref, wo_ref, pw_ref, ps_ref, wop_ref,
                      wout_ref, post_g_ref, *rest, seq_len):
    n_w = (len(rest) - 2) // 2
    out_ref, ext_ref = rest[n_w], rest[-1]
    _cast_stream_step(rest[:n_w], rest[n_w + 1:-1])
    i = pl.program_id(1)
    n_i = pl.num_programs(1)
    tm = pool_ref.shape[1]

    ext_ref[0:POOL_HALO, :] = jnp.where(i > 0, prev_ref[0], 0.0)
    ext_ref[POOL_HALO:POOL_HALO + tm, :] = pool_ref[0]
    ext_ref[POOL_HALO + tm:POOL_HALO + tm + POOL_HALO, :] = jnp.where(i < n_i - 1, next_ref[0], 0.0)

    win = MIX_SUB + 2 * POOL_HALO
    for r in range(tm // MIX_SUB):
        r0 = r * MIX_SUB
        rows = slice(r0, r0 + MIX_SUB)
        o_rows = o_ref[0, rows, :]
        t_first = i * tm + r0 + jax.lax.broadcasted_iota(jnp.int32, (POOL_HALO, 1), 0)
        t_last = t_first + (MIX_SUB - POOL_HALO)
        ys, y_attn_parts = [], []
        n_col = D_MODEL // len(POOL_WINDOWS)
        for gi, w in enumerate(POOL_WINDOWS):
            y_attn_parts.append(jnp.dot(o_rows, wo_ref[:, gi * n_col:(gi + 1) * n_col],
                                        preferred_element_type=jnp.float32))
            left = w // 2
            right = w - 1 - left
            cols = slice(gi * POOL_GROUP, (gi + 1) * POOL_GROUP)
            xe = ext_ref[r0:r0 + win, cols]
            fwd, span = xe, 1
            while span < min(w, POOL_HALO):
                fwd = fwd + pltpu.roll(fwd, win - span, axis=0)
                span *= 2
            if w > span:
                fwd = fwd + pltpu.roll(fwd, win - span, axis=0)
            total = pltpu.roll(fwd, left, axis=0)[POOL_HALO:POOL_HALO + MIX_SUB]
            inv = [1.0 / (jnp.minimum(tt + right + 1, seq_len) - jnp.maximum(tt - left, 0)).astype(jnp.float32)
                   for tt in (t_first, t_last)]
            mean = jnp.concatenate([total[:POOL_HALO] * inv[0], total[POOL_HALO:-POOL_HALO] * (1.0 / w),
                                    total[-POOL_HALO:] * inv[1]], axis=0)
            dg = mean - xe[POOL_HALO:POOL_HALO + MIX_SUB]
            ys.append(_bdot(dg, pw_ref[gi]))
        y = jnp.concatenate(ys, axis=1) * ps_ref[...]
        y_pool = _bdot(y, wop_ref[...])
        y_attn = jnp.concatenate(y_attn_parts, axis=1)

        g_attn = gate_ref[0, rows, 0:D_MODEL].astype(jnp.float32)
        g_pool = gate_ref[0, rows, D_MODEL:2 * D_MODEL].astype(jnp.float32)
        mixed = _bdot(g_attn * y_attn + g_pool * y_pool, wout_ref[...])
        out_ref[0, rows, :] = x_ref[0, rows, :] + _rmsnorm(mixed, post_g_ref[...])


def _mixer_out(o, pool, gate, x, wo, pw, ps, wop, wout, post_g, weights):
    b, s, d = x.shape
    tm = MIX_OUT_TM
    nb = tm // POOL_HALO
    last = s // POOL_HALO - 1
    n_i = s // tm
    w_in_specs, w_out_specs, w_shapes = _cast_stream_specs(weights, b * n_i, lambda bi, i: bi * n_i + i)
    rows = lambda w: pl.BlockSpec((1, tm, w), lambda bi, i: (bi, i, 0))
    prev = pl.BlockSpec((1, POOL_HALO, POOL_WIDTH), lambda bi, i: (bi, jnp.maximum(i * nb - 1, 0), 0))
    nxt = pl.BlockSpec((1, POOL_HALO, POOL_WIDTH), lambda bi, i: (bi, jnp.minimum((i + 1) * nb, last), 0))
    return pl.pallas_call(
        functools.partial(_mixer_out_kernel, seq_len=s),
        grid=(b, s // tm),
        in_specs=[rows(d), rows(POOL_WIDTH), prev, nxt, rows(2 * d), rows(d), _resident(wo.shape),
                  _resident(pw.shape), _resident(ps.shape), _resident(wop.shape), _resident(wout.shape),
                  _resident(post_g.shape)] + w_in_specs,
        out_specs=[rows(d)] + w_out_specs,
        out_shape=[jax.ShapeDtypeStruct((b, s, d), jnp.float32)] + w_shapes,
        scratch_shapes=[pltpu.VMEM((tm + 2 * POOL_HALO, POOL_WIDTH), jnp.float32)],
        compiler_params=pltpu.CompilerParams(dimension_semantics=("arbitrary", "arbitrary"),
                                             vmem_limit_bytes=VMEM_LIMIT),
        name="mixer_out",
    )(o, pool, pool, pool, gate, x, wo, pw, ps, wop, wout, post_g, *weights)


def _split_w_uq(w_uq):
    r = w_uq.shape[0]
    w3 = w_uq.reshape(r, N_HEADS, QK_DIM)
    nope = w3[:, :, :QK_NOPE_DIM].reshape(r, N_HEADS * QK_NOPE_DIM)
    return nope, w3[:, :, QK_NOPE_DIM:].reshape(r, N_HEADS * QK_ROPE_DIM)


def kernel(x, positions, ffn1_pre_g, ffn1_w_gate, ffn1_w_up, ffn1_w_down, ffn1_post_g, mix_pre_g, w_in, q_a_norm_g, w_uq, kv_a_norm_g, w_uk, w_uv, w_o_attn, pool_w, pool_scale, w_o_pool, w_out, mix_post_g, ffn2_pre_g, ffn2_w_gate, ffn2_w_up, ffn2_w_down, ffn2_post_g, final_g):
    b, s, d = x.shape
    bf = lambda w: w.astype(jnp.bfloat16)
    row = lambda g: g.reshape(1, -1)
    inv_freq = ROPE_THETA ** (-jnp.arange(0, QK_ROPE_DIM, 2, dtype=jnp.float32) / QK_ROPE_DIM)
    n_freq = QK_ROPE_DIM // 2
    invf = jnp.tile(inv_freq, LANES // n_freq).reshape(1, LANES)
    pos_rep = jnp.repeat(positions.reshape(b * s * n_freq // LANES, LANES // n_freq), n_freq, axis=1)
    assert w_in.shape[0] == 1, "one layer: the second FFN's weight casts are hosted by the mixer_out before it"
    l = 0

    x1, cos, sin, w_ext = _ffn(x.reshape(b * s, d), row(ffn1_pre_g[l]), ffn1_w_gate[l], ffn1_w_up[l], ffn1_w_down[l],
                               row(ffn1_post_g[l]), row(final_g[l]), False, rope=(pos_rep, invf), w_in=w_in[l])
    x1, cos, sin = x1.reshape(b, s, d), cos.reshape(b, s, LANES), sin.reshape(b, s, LANES)
    wq, wqr = _split_w_uq(bf(w_uq[l]))
    q, k, vt, pool, gate = _mixer_in(
        x1, cos, sin, row(mix_pre_g[l]), w_ext, row(q_a_norm_g[l]),
        wq, wqr, row(kv_a_norm_g[l]), bf(w_uk[l]), bf(w_uv[l]).T)
    o = _attention(q, k, vt.reshape(b, N_HEADS, V_HEAD_DIM, s))
    x2, *ffn2_w = _mixer_out(o, pool, gate, x1, bf(w_o_attn[l]), bf(pool_w[l]), row(pool_scale[l]),
                             bf(w_o_pool[l]), bf(w_out[l]), row(mix_post_g[l]),
                             [ffn2_w_gate[l], ffn2_w_up[l], ffn2_w_down[l]])
    return _ffn(x2.reshape(b * s, d), row(ffn2_pre_g[l]), *ffn2_w, row(ffn2_post_g[l]), row(final_g[l]),
                True).reshape(b, s, d)
```

```python
import functools
import math

import jax
import jax.numpy as jnp
from jax.experimental import pallas as pl
from jax.experimental.pallas import tpu as pltpu

D_MODEL = 1024
N_HEADS = 8
QK_NOPE_DIM = 128
QK_ROPE_DIM = 64
QK_DIM = QK_NOPE_DIM + QK_ROPE_DIM
V_HEAD_DIM = 128
Q_LORA_RANK = 384
KV_LORA_RANK = 256
ROPE_THETA = 10000.0
POOL_WINDOWS = (2, 4, 8, 16)
POOL_GROUP = 128
POOL_WIDTH = POOL_GROUP * len(POOL_WINDOWS)
D_FF = 2816
MACARON_WEIGHT = 0.5
NORM_EPS = 1e-6

LANES = 128
POOL_HALO = 8

OFF_CQ = 0
OFF_CKV = OFF_CQ + Q_LORA_RANK
OFF_KR = OFF_CKV + KV_LORA_RANK
OFF_POOL = OFF_KR + 2 * QK_ROPE_DIM
OFF_GATE = OFF_POOL + POOL_WIDTH
IN_WIDTH_EXT = OFF_GATE + 2 * D_MODEL
W_IN_ROWS = 64
FFN_W_STEPS = 11
ROPE_TABLE_ROWS = 512

FFN_TM = 1024
FFN_SUB = 256
MIX_IN_TM = 512
MIX_OUT_TM = 1024
MIX_SUB = 256
ATT_TQ = 512
ATT_KC = 256
ATT_HEADS = 4
VMEM_LIMIT = 56 * 1024 * 1024


def _resident(shape):
    return pl.BlockSpec(shape, lambda *_: (0,) * len(shape), pipeline_mode=pl.Buffered(1))


def _rmsnorm(x, g):
    ms = jnp.mean(x * x, axis=-1, keepdims=True)
    return x * jax.lax.rsqrt(ms + NORM_EPS) * g


def _bdot(a, b):
    return jnp.dot(a.astype(jnp.bfloat16), b, preferred_element_type=jnp.float32)


def _rope_table_block(pos_ref, invf_ref, cos_ref, sin_ref):
    ang = pos_ref[...].astype(jnp.float32) * invf_ref[...]
    n_freq = QK_ROPE_DIM // 2
    tok_per_row = LANES // n_freq
    rows = ang.shape[0]
    for table, out_ref in ((jnp.cos(ang), cos_ref), (jnp.sin(ang), sin_ref)):
        for k in range(tok_per_row):
            piece = table[:, k * n_freq:(k + 1) * n_freq]
            out_ref[pl.ds(k, rows, stride=tok_per_row), :] = jnp.concatenate([piece] * tok_per_row, axis=1)


def _ffn_kernel(x_ref, pre_g_ref, wg_ref, wu_ref, wd_ref, post_g_ref, final_g_ref, *rest, final_norm, w_steps,
                rope_steps):
    if rope_steps:
        pos_ref, invf_ref, o_ref, cos_ref, sin_ref, *w_scratch = rest
    else:
        o_ref, *w_scratch = rest
    weights = w_scratch if w_steps else (wg_ref, wu_ref, wd_ref)

    def row_tile():
        for r in range(x_ref.shape[0] // FFN_SUB):
            rows = slice(r * FFN_SUB, (r + 1) * FFN_SUB)
            x = x_ref[rows, :]
            xn = _rmsnorm(x, pre_g_ref[...]).astype(jnp.bfloat16)
            g = jnp.dot(xn, weights[0][...], preferred_element_type=jnp.float32)
            u = jnp.dot(xn, weights[1][...], preferred_element_type=jnp.float32)
            h = g * (1.0 / (1.0 + jnp.exp(-g))) * u
            f = _bdot(h, weights[2][...])
            y = x + MACARON_WEIGHT * _rmsnorm(f, post_g_ref[...])
            if final_norm:
                y = _rmsnorm(y, final_g_ref[...])
            o_ref[rows, :] = y

    if not w_steps:
        row_tile()
        return
    i = pl.program_id(0)
    fc = wg_ref.shape[1]
    for c in range(w_steps):
        @pl.when(i == c)
        def _(c=c):
            cols = slice(c * fc, (c + 1) * fc)
            weights[0][:, cols] = wg_ref[...].astype(jnp.bfloat16)
            weights[1][:, cols] = wu_ref[...].astype(jnp.bfloat16)
            weights[2][cols, :] = wd_ref[...].astype(jnp.bfloat16)
    if rope_steps:
        pl.when(i < rope_steps)(lambda: _rope_table_block(pos_ref, invf_ref, cos_ref, sin_ref))
    pl.when(i >= w_steps)(row_tile)


def _ffn(x2d, pre_g, wg, wu, wd, post_g, final_g, final_norm, rope=None):
    n, d = x2d.shape
    f = wg.shape[1]
    tm = FFN_TM
    w_steps = 0 if wg.dtype == jnp.bfloat16 else FFN_W_STEPS
    rope_steps, rope_in, rope_specs, rope_out_specs, rope_shapes = 0, [], [], [], []
    if rope is not None:
        pos_rep, invf = rope
        tr = ROPE_TABLE_ROWS
        tok_per_row = LANES // (QK_ROPE_DIM // 2)
        rope_steps = pos_rep.shape[0] // tr
        assert 0 < rope_steps <= w_steps
        blk = lambda i: (jnp.minimum(i, rope_steps - 1), 0)
        rope_in = [pos_rep, invf]
        rope_specs = [pl.BlockSpec((tr, LANES), blk), _resident(invf.shape)]
        rope_out_specs = [pl.BlockSpec((tr * tok_per_row, LANES), blk)] * 2
        rope_shapes = [jax.ShapeDtypeStruct((pos_rep.shape[0] * tok_per_row, LANES), jnp.float32)] * 2
    if w_steps:
        fc = f // w_steps
        chunk = lambda i: jnp.minimum(i, w_steps - 1)
        w_specs = [pl.BlockSpec((d, fc), lambda i: (0, chunk(i))), pl.BlockSpec((d, fc), lambda i: (0, chunk(i))),
                   pl.BlockSpec((fc, d), lambda i: (chunk(i), 0))]
        w_scratch = [pltpu.VMEM((d, f), jnp.bfloat16), pltpu.VMEM((d, f), jnp.bfloat16),
                     pltpu.VMEM((f, d), jnp.bfloat16)]
    else:
        w_specs, w_scratch = [_resident(wg.shape), _resident(wu.shape), _resident(wd.shape)], []
    row = pl.BlockSpec((tm, d), lambda i: (jnp.maximum(i - w_steps, 0), 0))
    outs = pl.pallas_call(
        functools.partial(_ffn_kernel, final_norm=final_norm, w_steps=w_steps, rope_steps=rope_steps),
        grid=(w_steps + n // tm,),
        in_specs=[row, _resident((1, d))] + w_specs + [_resident((1, d)), _resident((1, d))] + rope_specs,
        out_specs=[row] + rope_out_specs,
        out_shape=[jax.ShapeDtypeStruct((n, d), jnp.float32)] + rope_shapes,
        scratch_shapes=w_scratch,
        compiler_params=pltpu.CompilerParams(dimension_semantics=("arbitrary",), vmem_limit_bytes=VMEM_LIMIT),
        name="ffn_final" if final_norm else "ffn",
    )(x2d, pre_g, wg, wu, wd, post_g, final_g, *rope_in)
    return outs if rope_steps else outs[0]


def _cast_stream_specs(weights, n_steps, step_of):
    spec = lambda w: pl.BlockSpec((w.shape[0] // n_steps, w.shape[1]), lambda *g: (step_of(*g), 0))
    specs = [spec(w) for w in weights]
    return specs, specs, [jax.ShapeDtypeStruct(w.shape, jnp.bfloat16) for w in weights]


def _cast_stream_step(in_refs, out_refs):
    for src, dst in zip(in_refs, out_refs):
        dst[...] = src[...].astype(jnp.bfloat16)


def _extend_w_in(w_ref, wext_ref):
    kr1 = OFF_KR + QK_ROPE_DIM
    half = QK_ROPE_DIM // 2
    n_in = w_ref.shape[1]

    def body(rb, carry):
        rows = pl.ds(pl.multiple_of(rb * W_IN_ROWS, W_IN_ROWS), W_IN_ROWS)
        wext_ref[rows, 0:kr1] = w_ref[rows, 0:kr1].astype(jnp.bfloat16)
        wext_ref[rows, kr1:kr1 + half] = w_ref[rows, OFF_KR + half:kr1].astype(jnp.bfloat16)
        wext_ref[rows, kr1 + half:OFF_POOL] = w_ref[rows, OFF_KR:OFF_KR + half].astype(jnp.bfloat16)
        wext_ref[rows, OFF_POOL:IN_WIDTH_EXT] = w_ref[rows, kr1:n_in].astype(jnp.bfloat16)
        return carry

    jax.lax.fori_loop(0, w_ref.shape[0] // W_IN_ROWS, body, 0)


def _mixer_in_kernel(x_ref, cos_ref, sin_ref, pre_g_ref, w_in_ref, *refs):
    weights, (q_ref, k_ref, vt_ref, pool_ref, gate_ref), wext_ref = refs[:-6], refs[-6:-1], refs[-1]

    @pl.when((pl.program_id(0) == 0) & (pl.program_id(1) == 0))
    def _():
        _extend_w_in(w_in_ref, wext_ref)

    weights = (pre_g_ref, wext_ref) + tuple(weights)
    for r in range(x_ref.shape[1] // MIX_SUB):
        rows = pl.ds(r * MIX_SUB, MIX_SUB)
        _mixer_in_tile(x_ref.at[:, rows, :], cos_ref.at[:, rows, :], sin_ref.at[:, rows, :], *weights,
                       q_ref.at[:, :, rows, :], k_ref.at[:, :, rows, :], vt_ref.at[:, :, rows],
                       pool_ref.at[:, rows, :], gate_ref.at[:, rows, :])


def _mixer_in_tile(x_ref, cos_ref, sin_ref, pre_g_ref, w_in_ref, qg_ref, wq_ref, wqr_ref, kvg_ref, wuk_ref, wuvt_ref,
                   q_ref, k_ref, vt_ref, pool_ref, gate_ref):
    u = _rmsnorm(x_ref[0], pre_g_ref[...]).astype(jnp.bfloat16)
    z = jnp.dot(u, w_in_ref[:, :OFF_GATE], preferred_element_type=jnp.float32)
    pool_ref[0] = z[:, OFF_POOL:OFF_POOL + POOL_WIDTH]

    gl = jnp.dot(u, w_in_ref[:, OFF_GATE:], preferred_element_type=jnp.float32)
    gate_ref[0] = (1.0 / (1.0 + jnp.exp(-gl))).astype(jnp.bfloat16)

    cos = cos_ref[0]
    sin = sin_ref[0]
    lane = jax.lax.broadcasted_iota(jnp.int32, cos.shape, 1)
    first_half = (lane % QK_ROPE_DIM) < (QK_ROPE_DIM // 2)
    ssin = jnp.where(first_half, -sin, sin)

    kr2 = z[:, OFF_KR:OFF_KR + LANES] * jnp.where(lane < QK_ROPE_DIM, cos, ssin)
    k_rope = (kr2[:, :QK_ROPE_DIM] + kr2[:, QK_ROPE_DIM:]).astype(jnp.bfloat16)
    ckv = _rmsnorm(z[:, OFF_CKV:OFF_CKV + KV_LORA_RANK], kvg_ref[...]).astype(jnp.bfloat16)
    cq = _rmsnorm(z[:, OFF_CQ:OFF_CQ + Q_LORA_RANK], qg_ref[...]).astype(jnp.bfloat16)

    k_nope = jnp.dot(ckv, wuk_ref[...], preferred_element_type=jnp.float32).astype(jnp.bfloat16)
    for h in range(N_HEADS):
        k_ref[0, h, :, 0:QK_NOPE_DIM] = k_nope[:, h * QK_NOPE_DIM:(h + 1) * QK_NOPE_DIM]
        k_ref[0, h, :, QK_NOPE_DIM:QK_DIM] = k_rope
    vt_ref[0] = jax.lax.dot_general(wuvt_ref[...], ckv, (((1,), (1,)), ((), ())),
                                    preferred_element_type=jnp.float32).astype(jnp.bfloat16)

    scale = math.log2(math.e) / math.sqrt(QK_DIM)
    qr = jnp.dot(cq, wqr_ref[...], preferred_element_type=jnp.float32)
    n_rep = N_HEADS * QK_ROPE_DIM // LANES
    cos_h = jnp.concatenate([cos] * n_rep, axis=1)
    ssin_h = jnp.concatenate([ssin] * n_rep, axis=1)
    half = QK_ROPE_DIM // 2
    qs = jnp.where(jnp.concatenate([first_half] * n_rep, axis=1),
                   pltpu.roll(qr, qr.shape[1] - half, axis=1), pltpu.roll(qr, half, axis=1))
    q_rope = ((qr * cos_h + qs * ssin_h) * scale).astype(jnp.bfloat16)
    q_nope = (jnp.dot(cq, wq_ref[...], preferred_element_type=jnp.float32) * scale).astype(jnp.bfloat16)
    for h in range(N_HEADS):
        q_ref[0, h, :, 0:QK_NOPE_DIM] = q_nope[:, h * QK_NOPE_DIM:(h + 1) * QK_NOPE_DIM]
        q_ref[0, h, :, QK_NOPE_DIM:QK_DIM] = q_rope[:, h * QK_ROPE_DIM:(h + 1) * QK_ROPE_DIM]


def _mixer_in(x, cos, sin, pre_g, w_in, qg, wq, wqr, kvg, wuk, wuv):
    b, s, d = x.shape
    tm = MIX_IN_TM
    heads = lambda w: pl.BlockSpec((1, N_HEADS, tm, w), lambda bi, i: (bi, 0, i, 0))
    rows = lambda w: pl.BlockSpec((1, tm, w), lambda bi, i: (bi, i, 0))
    return pl.pallas_call(
        _mixer_in_kernel,
        grid=(b, s // tm),
        in_specs=[rows(d), rows(LANES), rows(LANES), _resident((1, d)),
                  pl.BlockSpec((None,) + w_in.shape[1:], lambda *_: (0, 0, 0), pipeline_mode=pl.Buffered(1)),
                  _resident(qg.shape),
                  _resident(wq.shape), _resident(wqr.shape), _resident(kvg.shape),
                  _resident(wuk.shape), _resident(wuv.shape)],
        out_specs=[heads(QK_DIM), heads(QK_DIM), pl.BlockSpec((1, N_HEADS * V_HEAD_DIM, tm), lambda bi, i: (bi, 0, i)), rows(POOL_WIDTH), rows(2 * D_MODEL)],
        out_shape=[jax.ShapeDtypeStruct((b, N_HEADS, s, QK_DIM), jnp.bfloat16),
                   jax.ShapeDtypeStruct((b, N_HEADS, s, QK_DIM), jnp.bfloat16),
                   jax.ShapeDtypeStruct((b, N_HEADS * V_HEAD_DIM, s), jnp.bfloat16),
                   jax.ShapeDtypeStruct((b, s, POOL_WIDTH), jnp.float32),
                   jax.ShapeDtypeStruct((b, s, 2 * D_MODEL), jnp.bfloat16)],
        scratch_shapes=[pltpu.VMEM((d, IN_WIDTH_EXT), jnp.bfloat16)],
        compiler_params=pltpu.CompilerParams(dimension_semantics=("arbitrary", "arbitrary"),
                                             vmem_limit_bytes=VMEM_LIMIT),
        name="mixer_in",
    )(x, cos, sin, pre_g, w_in, qg, wq, wqr, kvg, wuk, wuv)


def _attn_kernel(q_ref, k_ref, vt_ref, o_ref, st_ref, p_ref):
    n_heads, s_len = k_ref.shape[1], k_ref.shape[2]
    n_sub = s_len // ATT_TQ
    n_chunk = s_len // ATT_KC
    tiles = [(hh, j) for hh in range(n_heads) for j in range(n_sub)]

    def logits(t):
        hh, j = tiles[t]
        st = jax.lax.dot_general(k_ref[0, hh], q_ref[0, hh, j * ATT_TQ:(j + 1) * ATT_TQ, :],
                                 (((1,), (1,)), ((), ())), preferred_element_type=jnp.float32)
        st_ref[t % 2] = st
        return jnp.max(st, axis=0, keepdims=True)

    m_next = logits(0)
    for t, (hh, j) in enumerate(tiles):
        slot = t % 2
        m = m_next
        if t + 1 < len(tiles):
            m_next = logits(t + 1)
        l = jnp.zeros_like(m)
        for c in range(n_chunk):
            keys = slice(c * ATT_KC, (c + 1) * ATT_KC)
            p = jnp.exp2(st_ref[slot, keys, :] - m)
            l = l + jnp.sum(p, axis=0, keepdims=True)
            p_ref[slot, keys, :] = p.astype(jnp.bfloat16)
        ot = jnp.dot(vt_ref[0, hh], p_ref[slot], preferred_element_type=jnp.float32)
        o_ref[0, j * ATT_TQ:(j + 1) * ATT_TQ, hh * V_HEAD_DIM:(hh + 1) * V_HEAD_DIM] = (
            (ot / l).T.astype(jnp.bfloat16))


def _attention(q, k, vt):
    b, h, s, _ = q.shape
    hs = ATT_HEADS
    head = lambda r, c: pl.BlockSpec((1, hs, r, c), lambda bi, hi: (bi, hi, 0, 0))
    return pl.pallas_call(
        _attn_kernel,
        grid=(b, h // hs),
        in_specs=[head(s, QK_DIM), head(s, QK_DIM), head(V_HEAD_DIM, s)],
        out_specs=pl.BlockSpec((1, s, hs * V_HEAD_DIM), lambda bi, hi: (bi, 0, hi)),
        out_shape=jax.ShapeDtypeStruct((b, s, h * V_HEAD_DIM), jnp.bfloat16),
        scratch_shapes=[pltpu.VMEM((2, s, ATT_TQ), jnp.float32), pltpu.VMEM((2, s, ATT_TQ), jnp.bfloat16)],
        compiler_params=pltpu.CompilerParams(dimension_semantics=("arbitrary", "arbitrary"),
                                             vmem_limit_bytes=VMEM_LIMIT),
        name="attention",
    )(q, k, vt)


def _mixer_out_kernel(o_ref, pool_ref, prev_ref, next_ref, gate_ref, x_ref, wo_ref, pw_ref, ps_ref, wop_ref,
                      wout_ref, post_g_ref, *rest, seq_len):
    n_w = (len(rest) - 2) // 2
    out_ref, ext_ref = rest[n_w], rest[-1]
    _cast_stream_step(rest[:n_w], rest[n_w + 1:-1])
    i = pl.program_id(1)
    n_i = pl.num_programs(1)
    tm = pool_ref.shape[1]

    ext_ref[0:POOL_HALO, :] = jnp.where(i > 0, prev_ref[0], 0.0)
    ext_ref[POOL_HALO:POOL_HALO + tm, :] = pool_ref[0]
    ext_ref[POOL_HALO + tm:POOL_HALO + tm + POOL_HALO, :] = jnp.where(i < n_i - 1, next_ref[0], 0.0)

    win = MIX_SUB + 2 * POOL_HALO
    for r in range(tm // MIX_SUB):
        r0 = r * MIX_SUB
        rows = slice(r0, r0 + MIX_SUB)
        o_rows = o_ref[0, rows, :]
        t_first = i * tm + r0 + jax.lax.broadcasted_iota(jnp.int32, (POOL_HALO, 1), 0)
        t_last = t_first + (MIX_SUB - POOL_HALO)
        ys, y_attn_parts = [], []
        n_col = D_MODEL // len(POOL_WINDOWS)
        for gi, w in enumerate(POOL_WINDOWS):
            y_attn_parts.append(jnp.dot(o_rows, wo_ref[:, gi * n_col:(gi + 1) * n_col],
                                        preferred_element_type=jnp.float32))
            left = w // 2
            right = w - 1 - left
            cols = slice(gi * POOL_GROUP, (gi + 1) * POOL_GROUP)
            xe = ext_ref[r0:r0 + win, cols]
            fwd, span = xe, 1
            while span < min(w, POOL_HALO):
                fwd = fwd + pltpu.roll(fwd, win - span, axis=0)
                span *= 2
            if w > span:
                fwd = fwd + pltpu.roll(fwd, win - span, axis=0)
            total = pltpu.roll(fwd, left, axis=0)[POOL_HALO:POOL_HALO + MIX_SUB]
            inv = [1.0 / (jnp.minimum(tt + right + 1, seq_len) - jnp.maximum(tt - left, 0)).astype(jnp.float32)
                   for tt in (t_first, t_last)]
            mean = jnp.concatenate([total[:POOL_HALO] * inv[0], total[POOL_HALO:-POOL_HALO] * (1.0 / w),
                                    total[-POOL_HALO:] * inv[1]], axis=0)
            dg = mean - xe[POOL_HALO:POOL_HALO + MIX_SUB]
            ys.append(_bdot(dg, pw_ref[gi]))
        y = jnp.concatenate(ys, axis=1) * ps_ref[...]
        y_pool = _bdot(y, wop_ref[...])
        y_attn = jnp.concatenate(y_attn_parts, axis=1)

        g_attn = gate_ref[0, rows, 0:D_MODEL].astype(jnp.float32)
        g_pool = gate_ref[0, rows, D_MODEL:2 * D_MODEL].astype(jnp.float32)
        mixed = _bdot(g_attn * y_attn + g_pool * y_pool, wout_ref[...])
        out_ref[0, rows, :] = x_ref[0, rows, :] + _rmsnorm(mixed, post_g_ref[...])


def _mixer_out(o, pool, gate, x, wo, pw, ps, wop, wout, post_g, weights):
    b, s, d = x.shape
    tm = MIX_OUT_TM
    nb = tm // POOL_HALO
    last = s // POOL_HALO - 1
    n_i = s // tm
    w_in_specs, w_out_specs, w_shapes = _cast_stream_specs(weights, b * n_i, lambda bi, i: bi * n_i + i)
    rows = lambda w: pl.BlockSpec((1, tm, w), lambda bi, i: (bi, i, 0))
    prev = pl.BlockSpec((1, POOL_HALO, POOL_WIDTH), lambda bi, i: (bi, jnp.maximum(i * nb - 1, 0), 0))
    nxt = pl.BlockSpec((1, POOL_HALO, POOL_WIDTH), lambda bi, i: (bi, jnp.minimum((i + 1) * nb, last), 0))
    return pl.pallas_call(
        functools.partial(_mixer_out_kernel, seq_len=s),
        grid=(b, s // tm),
        in_specs=[rows(d), rows(POOL_WIDTH), prev, nxt, rows(2 * d), rows(d), _resident(wo.shape),
                  _resident(pw.shape), _resident(ps.shape), _resident(wop.shape), _resident(wout.shape),
                  _resident(post_g.shape)] + w_in_specs,
        out_specs=[rows(d)] + w_out_specs,
        out_shape=[jax.ShapeDtypeStruct((b, s, d), jnp.float32)] + w_shapes,
        scratch_shapes=[pltpu.VMEM((tm + 2 * POOL_HALO, POOL_WIDTH), jnp.float32)],
        compiler_params=pltpu.CompilerParams(dimension_semantics=("arbitrary", "arbitrary"),
                                             vmem_limit_bytes=VMEM_LIMIT),
        name="mixer_out",
    )(o, pool, pool, pool, gate, x, wo, pw, ps, wop, wout, post_g, *weights)


def _split_w_uq(w_uq):
    r = w_uq.shape[0]
    w3 = w_uq.reshape(r, N_HEADS, QK_DIM)
    nope = w3[:, :, :QK_NOPE_DIM].reshape(r, N_HEADS * QK_NOPE_DIM)
    return nope, w3[:, :, QK_NOPE_DIM:].reshape(r, N_HEADS * QK_ROPE_DIM)


def kernel(x, positions, ffn1_pre_g, ffn1_w_gate, ffn1_w_up, ffn1_w_down, ffn1_post_g, mix_pre_g, w_in, q_a_norm_g, w_uq, kv_a_norm_g, w_uk, w_uv, w_o_attn, pool_w, pool_scale, w_o_pool, w_out, mix_post_g, ffn2_pre_g, ffn2_w_gate, ffn2_w_up, ffn2_w_down, ffn2_post_g, final_g):
    b, s, d = x.shape
    bf = lambda w: w.astype(jnp.bfloat16)
    row = lambda g: g.reshape(1, -1)
    inv_freq = ROPE_THETA ** (-jnp.arange(0, QK_ROPE_DIM, 2, dtype=jnp.float32) / QK_ROPE_DIM)
    n_freq = QK_ROPE_DIM // 2
    invf = jnp.tile(inv_freq, LANES // n_freq).reshape(1, LANES)
    pos_rep = jnp.repeat(positions.reshape(b * s * n_freq // LANES, LANES // n_freq), n_freq, axis=1)
    assert w_in.shape[0] == 1, "one layer: the second FFN's weight casts are hosted by the mixer_out before it"
    l = 0

    x1, cos, sin = _ffn(x.reshape(b * s, d), row(ffn1_pre_g[l]), ffn1_w_gate[l], ffn1_w_up[l], ffn1_w_down[l],
                        row(ffn1_post_g[l]), row(final_g[l]), False, rope=(pos_rep, invf))
    x1, cos, sin = x1.reshape(b, s, d), cos.reshape(b, s, LANES), sin.reshape(b, s, LANES)
    wq, wqr = _split_w_uq(bf(w_uq[l]))
    q, k, vt, pool, gate = _mixer_in(
        x1, cos, sin, row(mix_pre_g[l]), w_in, row(q_a_norm_g[l]),
        wq, wqr, row(kv_a_norm_g[l]), bf(w_uk[l]), bf(w_uv[l]).T)
    o = _attention(q, k, vt.reshape(b, N_HEADS, V_HEAD_DIM, s))
    x2, *ffn2_w = _mixer_out(o, pool, gate, x1, bf(w_o_attn[l]), bf(pool_w[l]), row(pool_scale[l]),
                             bf(w_o_pool[l]), bf(w_out[l]), row(mix_post_g[l]),
                             [ffn2_w_gate[l], ffn2_w_up[l], ffn2_w_down[l]])
    return _ffn(x2.reshape(b * s, d), row(ffn2_pre_g[l]), *ffn2_w, row(ffn2_post_g[l]), row(final_g[l]),
                True).reshape(b, s, d)
```

```python
import functools
import math

import jax
import jax.numpy as jnp
from jax.experimental import pallas as pl
from jax.experimental.pallas import tpu as pltpu

D_MODEL = 1024
N_HEADS = 8
QK_NOPE_DIM = 128
QK_ROPE_DIM = 64
QK_DIM = QK_NOPE_DIM + QK_ROPE_DIM
V_HEAD_DIM = 128
Q_LORA_RANK = 384
KV_LORA_RANK = 256
ROPE_THETA = 10000.0
POOL_WINDOWS = (2, 4, 8, 16)
POOL_GROUP = 128
POOL_WIDTH = POOL_GROUP * len(POOL_WINDOWS)
D_FF = 2816
MACARON_WEIGHT = 0.5
NORM_EPS = 1e-6

LANES = 128
POOL_HALO = 8

OFF_CQ = 0
OFF_CKV = OFF_CQ + Q_LORA_RANK
OFF_KR = OFF_CKV + KV_LORA_RANK
OFF_POOL = OFF_KR + 2 * QK_ROPE_DIM
OFF_GATE = OFF_POOL + POOL_WIDTH
IN_WIDTH_EXT = OFF_GATE + 2 * D_MODEL
W_IN_ROWS = 64
FFN_W_STEPS = 11
ROPE_TABLE_ROWS = 512

FFN_TM = 1024
FFN_SUB = 256
MIX_IN_TM = 512
MIX_OUT_TM = 1024
MIX_SUB = 256
ATT_TQ = 512
ATT_KC = 256
ATT_HEADS = 4
VMEM_LIMIT = 56 * 1024 * 1024


def _resident(shape):
    return pl.BlockSpec(shape, lambda *_: (0,) * len(shape), pipeline_mode=pl.Buffered(1))


def _rmsnorm(x, g):
    ms = jnp.mean(x * x, axis=-1, keepdims=True)
    return x * jax.lax.rsqrt(ms + NORM_EPS) * g


def _bdot(a, b):
    return jnp.dot(a.astype(jnp.bfloat16), b, preferred_element_type=jnp.float32)


def _rope_table_block(pos_ref, invf_ref, cos_ref, sin_ref):
    ang = pos_ref[...].astype(jnp.float32) * invf_ref[...]
    n_freq = QK_ROPE_DIM // 2
    tok_per_row = LANES // n_freq
    rows = ang.shape[0]
    for table, out_ref in ((jnp.cos(ang), cos_ref), (jnp.sin(ang), sin_ref)):
        for k in range(tok_per_row):
            piece = table[:, k * n_freq:(k + 1) * n_freq]
            out_ref[pl.ds(k, rows, stride=tok_per_row), :] = jnp.concatenate([piece] * tok_per_row, axis=1)


def _ffn_kernel(x_ref, pre_g_ref, wg_ref, wu_ref, wd_ref, post_g_ref, final_g_ref, *rest, final_norm, w_steps,
                rope_steps):
    if rope_steps:
        pos_ref, invf_ref, o_ref, cos_ref, sin_ref, *w_scratch = rest
    else:
        o_ref, *w_scratch = rest
    weights = w_scratch if w_steps else (wg_ref, wu_ref, wd_ref)

    def row_tile():
        for r in range(x_ref.shape[0] // FFN_SUB):
            rows = slice(r * FFN_SUB, (r + 1) * FFN_SUB)
            x = x_ref[rows, :]
            xn = _rmsnorm(x, pre_g_ref[...]).astype(jnp.bfloat16)
            g = jnp.dot(xn, weights[0][...], preferred_element_type=jnp.float32)
            u = jnp.dot(xn, weights[1][...], preferred_element_type=jnp.float32)
            h = g * (1.0 / (1.0 + jnp.exp(-g))) * u
            f = _bdot(h, weights[2][...])
            y = x + MACARON_WEIGHT * _rmsnorm(f, post_g_ref[...])
            if final_norm:
                y = _rmsnorm(y, final_g_ref[...])
            o_ref[rows, :] = y

    if not w_steps:
        row_tile()
        return
    i = pl.program_id(0)
    fc = wg_ref.shape[1]
    for c in range(w_steps):
        @pl.when(i == c)
        def _(c=c):
            cols = slice(c * fc, (c + 1) * fc)
            weights[0][:, cols] = wg_ref[...].astype(jnp.bfloat16)
            weights[1][:, cols] = wu_ref[...].astype(jnp.bfloat16)
            weights[2][cols, :] = wd_ref[...].astype(jnp.bfloat16)
    if rope_steps:
        pl.when(i < rope_steps)(lambda: _rope_table_block(pos_ref, invf_ref, cos_ref, sin_ref))
    pl.when(i >= w_steps)(row_tile)


def _ffn(x2d, pre_g, wg, wu, wd, post_g, final_g, final_norm, rope=None):
    n, d = x2d.shape
    f = wg.shape[1]
    tm = FFN_TM
    w_steps = 0 if wg.dtype == jnp.bfloat16 else FFN_W_STEPS
    rope_steps, rope_in, rope_specs, rope_out_specs, rope_shapes = 0, [], [], [], []
    if rope is not None:
        pos_rep, invf = rope
        tr = ROPE_TABLE_ROWS
        tok_per_row = LANES // (QK_ROPE_DIM // 2)
        rope_steps = pos_rep.shape[0] // tr
        assert 0 < rope_steps <= w_steps
        blk = lambda i: (jnp.minimum(i, rope_steps - 1), 0)
        rope_in = [pos_rep, invf]
        rope_specs = [pl.BlockSpec((tr, LANES), blk), _resident(invf.shape)]
        rope_out_specs = [pl.BlockSpec((tr * tok_per_row, LANES), blk)] * 2
        rope_shapes = [jax.ShapeDtypeStruct((pos_rep.shape[0] * tok_per_row, LANES), jnp.float32)] * 2
    if w_steps:
        fc = f // w_steps
        chunk = lambda i: jnp.minimum(i, w_steps - 1)
        w_specs = [pl.BlockSpec((d, fc), lambda i: (0, chunk(i))), pl.BlockSpec((d, fc), lambda i: (0, chunk(i))),
                   pl.BlockSpec((fc, d), lambda i: (chunk(i), 0))]
        w_scratch = [pltpu.VMEM((d, f), jnp.bfloat16), pltpu.VMEM((d, f), jnp.bfloat16),
                     pltpu.VMEM((f, d), jnp.bfloat16)]
    else:
        w_specs, w_scratch = [_resident(wg.shape), _resident(wu.shape), _resident(wd.shape)], []
    row = pl.BlockSpec((tm, d), lambda i: (jnp.maximum(i - w_steps, 0), 0))
    outs = pl.pallas_call(
        functools.partial(_ffn_kernel, final_norm=final_norm, w_steps=w_steps, rope_steps=rope_steps),
        grid=(w_steps + n // tm,),
        in_specs=[row, _resident((1, d))] + w_specs + [_resident((1, d)), _resident((1, d))] + rope_specs,
        out_specs=[row] + rope_out_specs,
        out_shape=[jax.ShapeDtypeStruct((n, d), jnp.float32)] + rope_shapes,
        scratch_shapes=w_scratch,
        compiler_params=pltpu.CompilerParams(dimension_semantics=("arbitrary",), vmem_limit_bytes=VMEM_LIMIT),
        name="ffn_final" if final_norm else "ffn",
    )(x2d, pre_g, wg, wu, wd, post_g, final_g, *rope_in)
    return outs if rope_steps else outs[0]


def _cast_stream_specs(weights, n_steps, step_of):
    spec = lambda w: pl.BlockSpec((w.shape[0] // n_steps, w.shape[1]), lambda *g: (step_of(*g), 0))
    specs = [spec(w) for w in weights]
    return specs, specs, [jax.ShapeDtypeStruct(w.shape, jnp.bfloat16) for w in weights]


def _cast_stream_step(in_refs, out_refs):
    for src, dst in zip(in_refs, out_refs):
        dst[...] = src[...].astype(jnp.bfloat16)


def _extend_w_in(w_ref, wext_ref):
    kr1 = OFF_KR + QK_ROPE_DIM
    half = QK_ROPE_DIM // 2
    n_in = w_ref.shape[1]

    def body(rb, carry):
        rows = pl.ds(pl.multiple_of(rb * W_IN_ROWS, W_IN_ROWS), W_IN_ROWS)
        wext_ref[rows, 0:kr1] = w_ref[rows, 0:kr1].astype(jnp.bfloat16)
        wext_ref[rows, kr1:kr1 + half] = w_ref[rows, OFF_KR + half:kr1].astype(jnp.bfloat16)
        wext_ref[rows, kr1 + half:OFF_POOL] = w_ref[rows, OFF_KR:OFF_KR + half].astype(jnp.bfloat16)
        wext_ref[rows, OFF_POOL:IN_WIDTH_EXT] = w_ref[rows, kr1:n_in].astype(jnp.bfloat16)
        return carry

    jax.lax.fori_loop(0, w_ref.shape[0] // W_IN_ROWS, body, 0)


def _mixer_in_kernel(x_ref, cos_ref, sin_ref, pre_g_ref, w_in_ref, *refs):
    weights, (q_ref, k_ref, vt_ref, pool_ref, gate_ref), (wext_ref, w_f32_ref) = refs[:-7], refs[-7:-2], refs[-2:]

    @pl.when((pl.program_id(0) == 0) & (pl.program_id(1) == 0))
    def _():
        pltpu.sync_copy(w_in_ref.at[0], w_f32_ref)
        _extend_w_in(w_f32_ref, wext_ref)

    weights = (pre_g_ref, wext_ref) + tuple(weights)
    for r in range(x_ref.shape[1] // MIX_SUB):
        rows = pl.ds(r * MIX_SUB, MIX_SUB)
        _mixer_in_tile(x_ref.at[:, rows, :], cos_ref.at[:, rows, :], sin_ref.at[:, rows, :], *weights,
                       q_ref.at[:, :, rows, :], k_ref.at[:, :, rows, :], vt_ref.at[:, :, rows],
                       pool_ref.at[:, rows, :], gate_ref.at[:, rows, :])


def _mixer_in_tile(x_ref, cos_ref, sin_ref, pre_g_ref, w_in_ref, qg_ref, wq_ref, wqr_ref, kvg_ref, wuk_ref, wuvt_ref,
                   q_ref, k_ref, vt_ref, pool_ref, gate_ref):
    u = _rmsnorm(x_ref[0], pre_g_ref[...]).astype(jnp.bfloat16)
    z = jnp.dot(u, w_in_ref[:, :OFF_GATE], preferred_element_type=jnp.float32)
    pool_ref[0] = z[:, OFF_POOL:OFF_POOL + POOL_WIDTH]

    gl = jnp.dot(u, w_in_ref[:, OFF_GATE:], preferred_element_type=jnp.float32)
    gate_ref[0] = (1.0 / (1.0 + jnp.exp(-gl))).astype(jnp.bfloat16)

    cos = cos_ref[0]
    sin = sin_ref[0]
    lane = jax.lax.broadcasted_iota(jnp.int32, cos.shape, 1)
    first_half = (lane % QK_ROPE_DIM) < (QK_ROPE_DIM // 2)
    ssin = jnp.where(first_half, -sin, sin)

    kr2 = z[:, OFF_KR:OFF_KR + LANES] * jnp.where(lane < QK_ROPE_DIM, cos, ssin)
    k_rope = (kr2[:, :QK_ROPE_DIM] + kr2[:, QK_ROPE_DIM:]).astype(jnp.bfloat16)
    ckv = _rmsnorm(z[:, OFF_CKV:OFF_CKV + KV_LORA_RANK], kvg_ref[...]).astype(jnp.bfloat16)
    cq = _rmsnorm(z[:, OFF_CQ:OFF_CQ + Q_LORA_RANK], qg_ref[...]).astype(jnp.bfloat16)

    k_nope = jnp.dot(ckv, wuk_ref[...], preferred_element_type=jnp.float32).astype(jnp.bfloat16)
    for h in range(N_HEADS):
        k_ref[0, h, :, 0:QK_NOPE_DIM] = k_nope[:, h * QK_NOPE_DIM:(h + 1) * QK_NOPE_DIM]
        k_ref[0, h, :, QK_NOPE_DIM:QK_DIM] = k_rope
    vt_ref[0] = jax.lax.dot_general(wuvt_ref[...], ckv, (((1,), (1,)), ((), ())),
                                    preferred_element_type=jnp.float32).astype(jnp.bfloat16)

    scale = math.log2(math.e) / math.sqrt(QK_DIM)
    qr = jnp.dot(cq, wqr_ref[...], preferred_element_type=jnp.float32)
    n_rep = N_HEADS * QK_ROPE_DIM // LANES
    cos_h = jnp.concatenate([cos] * n_rep, axis=1)
    ssin_h = jnp.concatenate([ssin] * n_rep, axis=1)
    half = QK_ROPE_DIM // 2
    qs = jnp.where(jnp.concatenate([first_half] * n_rep, axis=1),
                   pltpu.roll(qr, qr.shape[1] - half, axis=1), pltpu.roll(qr, half, axis=1))
    q_rope = ((qr * cos_h + qs * ssin_h) * scale).astype(jnp.bfloat16)
    q_nope = (jnp.dot(cq, wq_ref[...], preferred_element_type=jnp.float32) * scale).astype(jnp.bfloat16)
    for h in range(N_HEADS):
        q_ref[0, h, :, 0:QK_NOPE_DIM] = q_nope[:, h * QK_NOPE_DIM:(h + 1) * QK_NOPE_DIM]
        q_ref[0, h, :, QK_NOPE_DIM:QK_DIM] = q_rope[:, h * QK_ROPE_DIM:(h + 1) * QK_ROPE_DIM]


def _mixer_in(x, cos, sin, pre_g, w_in, qg, wq, wqr, kvg, wuk, wuv):
    b, s, d = x.shape
    tm = MIX_IN_TM
    heads = lambda w: pl.BlockSpec((1, N_HEADS, tm, w), lambda bi, i: (bi, 0, i, 0))
    rows = lambda w: pl.BlockSpec((1, tm, w), lambda bi, i: (bi, i, 0))
    return pl.pallas_call(
        _mixer_in_kernel,
        grid=(b, s // tm),
        in_specs=[rows(d), rows(LANES), rows(LANES), _resident((1, d)),
                  pl.BlockSpec(memory_space=pl.ANY),
                  _resident(qg.shape),
                  _resident(wq.shape), _resident(wqr.shape), _resident(kvg.shape),
                  _resident(wuk.shape), _resident(wuv.shape)],
        out_specs=[heads(QK_DIM), heads(QK_DIM), pl.BlockSpec((1, N_HEADS * V_HEAD_DIM, tm), lambda bi, i: (bi, 0, i)), rows(POOL_WIDTH), rows(2 * D_MODEL)],
        out_shape=[jax.ShapeDtypeStruct((b, N_HEADS, s, QK_DIM), jnp.bfloat16),
                   jax.ShapeDtypeStruct((b, N_HEADS, s, QK_DIM), jnp.bfloat16),
                   jax.ShapeDtypeStruct((b, N_HEADS * V_HEAD_DIM, s), jnp.bfloat16),
                   jax.ShapeDtypeStruct((b, s, POOL_WIDTH), jnp.float32),
                   jax.ShapeDtypeStruct((b, s, 2 * D_MODEL), jnp.bfloat16)],
        scratch_shapes=[pltpu.VMEM((d, IN_WIDTH_EXT), jnp.bfloat16), pltpu.VMEM(w_in.shape[1:], jnp.float32)],
        compiler_params=pltpu.CompilerParams(dimension_semantics=("arbitrary", "arbitrary"),
                                             vmem_limit_bytes=VMEM_LIMIT),
        name="mixer_in",
    )(x, cos, sin, pre_g, w_in, qg, wq, wqr, kvg, wuk, wuv)


def _attn_kernel(q_ref, k_ref, vt_ref, o_ref, st_ref, p_ref):
    n_heads, s_len = k_ref.shape[1], k_ref.shape[2]
    n_sub = s_len // ATT_TQ
    n_chunk = s_len // ATT_KC
    tiles = [(hh, j) for hh in range(n_heads) for j in range(n_sub)]

    def logits(t):
        hh, j = tiles[t]
        st = jax.lax.dot_general(k_ref[0, hh], q_ref[0, hh, j * ATT_TQ:(j + 1) * ATT_TQ, :],
                                 (((1,), (1,)), ((), ())), preferred_element_type=jnp.float32)
        st_ref[t % 2] = st
        return jnp.max(st, axis=0, keepdims=True)

    m_next = logits(0)
    for t, (hh, j) in enumerate(tiles):
        slot = t % 2
        m = m_next
        if t + 1 < len(tiles):
            m_next = logits(t + 1)
        l = jnp.zeros_like(m)
        for c in range(n_chunk):
            keys = slice(c * ATT_KC, (c + 1) * ATT_KC)
            p = jnp.exp2(st_ref[slot, keys, :] - m)
            l = l + jnp.sum(p, axis=0, keepdims=True)
            p_ref[slot, keys, :] = p.astype(jnp.bfloat16)
        ot = jnp.dot(vt_ref[0, hh], p_ref[slot], preferred_element_type=jnp.float32)
        o_ref[0, j * ATT_TQ:(j + 1) * ATT_TQ, hh * V_HEAD_DIM:(hh + 1) * V_HEAD_DIM] = (
            (ot / l).T.astype(jnp.bfloat16))


def _attention(q, k, vt):
    b, h, s, _ = q.shape
    hs = ATT_HEADS
    head = lambda r, c: pl.BlockSpec((1, hs, r, c), lambda bi, hi: (bi, hi, 0, 0))
    return pl.pallas_call(
        _attn_kernel,
        grid=(b, h // hs),
        in_specs=[head(s, QK_DIM), head(s, QK_DIM), head(V_HEAD_DIM, s)],
        out_specs=pl.BlockSpec((1, s, hs * V_HEAD_DIM), lambda bi, hi: (bi, 0, hi)),
        out_shape=jax.ShapeDtypeStruct((b, s, h * V_HEAD_DIM), jnp.bfloat16),
        scratch_shapes=[pltpu.VMEM((2, s, ATT_TQ), jnp.float32), pltpu.VMEM((2, s, ATT_TQ), jnp.bfloat16)],
        compiler_params=pltpu.CompilerParams(dimension_semantics=("arbitrary", "arbitrary"),
                                             vmem_limit_bytes=VMEM_LIMIT),
        name="attention",
    )(q, k, vt)


def _mixer_out_kernel(o_ref, pool_ref, prev_ref, next_ref, gate_ref, x_ref, wo_ref, pw_ref, ps_ref, wop_ref,
                      wout_ref, post_g_ref, *rest, seq_len):
    n_w = (len(rest) - 2) // 2
    out_ref, ext_ref = rest[n_w], rest[-1]
    _cast_stream_step(rest[:n_w], rest[n_w + 1:-1])
    i = pl.program_id(1)
    n_i = pl.num_programs(1)
    tm = pool_ref.shape[1]

    ext_ref[0:POOL_HALO, :] = jnp.where(i > 0, prev_ref[0], 0.0)
    ext_ref[POOL_HALO:POOL_HALO + tm, :] = pool_ref[0]
    ext_ref[POOL_HALO + tm:POOL_HALO + tm + POOL_HALO, :] = jnp.where(i < n_i - 1, next_ref[0], 0.0)

    win = MIX_SUB + 2 * POOL_HALO
    for r in range(tm // MIX_SUB):
        r0 = r * MIX_SUB
        rows = slice(r0, r0 + MIX_SUB)
        o_rows = o_ref[0, rows, :]
        t_first = i * tm + r0 + jax.lax.broadcasted_iota(jnp.int32, (POOL_HALO, 1), 0)
        t_last = t_first + (MIX_SUB - POOL_HALO)
        ys, y_attn_parts = [], []
        n_col = D_MODEL // len(POOL_WINDOWS)
        for gi, w in enumerate(POOL_WINDOWS):
            y_attn_parts.append(jnp.dot(o_rows, wo_ref[:, gi * n_col:(gi + 1) * n_col],
                                        preferred_element_type=jnp.float32))
            left = w // 2
            right = w - 1 - left
            cols = slice(gi * POOL_GROUP, (gi + 1) * POOL_GROUP)
            xe = ext_ref[r0:r0 + win, cols]
            fwd, span = xe, 1
            while span < min(w, POOL_HALO):
                fwd = fwd + pltpu.roll(fwd, win - span, axis=0)
                span *= 2
            if w > span:
                fwd = fwd + pltpu.roll(fwd, win - span, axis=0)
            total = pltpu.roll(fwd, left, axis=0)[POOL_HALO:POOL_HALO + MIX_SUB]
            inv = [1.0 / (jnp.minimum(tt + right + 1, seq_len) - jnp.maximum(tt - left, 0)).astype(jnp.float32)
                   for tt in (t_first, t_last)]
            mean = jnp.concatenate([total[:POOL_HALO] * inv[0], total[POOL_HALO:-POOL_HALO] * (1.0 / w),
                                    total[-POOL_HALO:] * inv[1]], axis=0)
            dg = mean - xe[POOL_HALO:POOL_HALO + MIX_SUB]
            ys.append(_bdot(dg, pw_ref[gi]))
        y = jnp.concatenate(ys, axis=1) * ps_ref[...]
        y_pool = _bdot(y, wop_ref[...])
        y_attn = jnp.concatenate(y_attn_parts, axis=1)

        g_attn = gate_ref[0, rows, 0:D_MODEL].astype(jnp.float32)
        g_pool = gate_ref[0, rows, D_MODEL:2 * D_MODEL].astype(jnp.float32)
        mixed = _bdot(g_attn * y_attn + g_pool * y_pool, wout_ref[...])
        out_ref[0, rows, :] = x_ref[0, rows, :] + _rmsnorm(mixed, post_g_ref[...])


def _mixer_out(o, pool, gate, x, wo, pw, ps, wop, wout, post_g, weights):
    b, s, d = x.shape
    tm = MIX_OUT_TM
    nb = tm // POOL_HALO
    last = s // POOL_HALO - 1
    n_i = s // tm
    w_in_specs, w_out_specs, w_shapes = _cast_stream_specs(weights, b * n_i, lambda bi, i: bi * n_i + i)
    rows = lambda w: pl.BlockSpec((1, tm, w), lambda bi, i: (bi, i, 0))
    prev = pl.BlockSpec((1, POOL_HALO, POOL_WIDTH), lambda bi, i: (bi, jnp.maximum(i * nb - 1, 0), 0))
    nxt = pl.BlockSpec((1, POOL_HALO, POOL_WIDTH), lambda bi, i: (bi, jnp.minimum((i + 1) * nb, last), 0))
    return pl.pallas_call(
        functools.partial(_mixer_out_kernel, seq_len=s),
        grid=(b, s // tm),
        in_specs=[rows(d), rows(POOL_WIDTH), prev, nxt, rows(2 * d), rows(d), _resident(wo.shape),
                  _resident(pw.shape), _resident(ps.shape), _resident(wop.shape), _resident(wout.shape),
                  _resident(post_g.shape)] + w_in_specs,
        out_specs=[rows(d)] + w_out_specs,
        out_shape=[jax.ShapeDtypeStruct((b, s, d), jnp.float32)] + w_shapes,
        scratch_shapes=[pltpu.VMEM((tm + 2 * POOL_HALO, POOL_WIDTH), jnp.float32)],
        compiler_params=pltpu.CompilerParams(dimension_semantics=("arbitrary", "arbitrary"),
                                             vmem_limit_bytes=VMEM_LIMIT),
        name="mixer_out",
    )(o, pool, pool, pool, gate, x, wo, pw, ps, wop, wout, post_g, *weights)


def _split_w_uq(w_uq):
    r = w_uq.shape[0]
    w3 = w_uq.reshape(r, N_HEADS, QK_DIM)
    nope = w3[:, :, :QK_NOPE_DIM].reshape(r, N_HEADS * QK_NOPE_DIM)
    return nope, w3[:, :, QK_NOPE_DIM:].reshape(r, N_HEADS * QK_ROPE_DIM)


def kernel(x, positions, ffn1_pre_g, ffn1_w_gate, ffn1_w_up, ffn1_w_down, ffn1_post_g, mix_pre_g, w_in, q_a_norm_g, w_uq, kv_a_norm_g, w_uk, w_uv, w_o_attn, pool_w, pool_scale, w_o_pool, w_out, mix_post_g, ffn2_pre_g, ffn2_w_gate, ffn2_w_up, ffn2_w_down, ffn2_post_g, final_g):
    b, s, d = x.shape
    bf = lambda w: w.astype(jnp.bfloat16)
    row = lambda g: g.reshape(1, -1)
    inv_freq = ROPE_THETA ** (-jnp.arange(0, QK_ROPE_DIM, 2, dtype=jnp.float32) / QK_ROPE_DIM)
    n_freq = QK_ROPE_DIM // 2
    invf = jnp.tile(inv_freq, LANES // n_freq).reshape(1, LANES)
    pos_rep = jnp.repeat(positions.reshape(b * s * n_freq // LANES, LANES // n_freq), n_freq, axis=1)
    assert w_in.shape[0] == 1, "one layer: the second FFN's weight casts are hosted by the mixer_out before it"
    l = 0

    x1, cos, sin = _ffn(x.reshape(b * s, d), row(ffn1_pre_g[l]), ffn1_w_gate[l], ffn1_w_up[l], ffn1_w_down[l],
                        row(ffn1_post_g[l]), row(final_g[l]), False, rope=(pos_rep, invf))
    x1, cos, sin = x1.reshape(b, s, d), cos.reshape(b, s, LANES), sin.reshape(b, s, LANES)
    wq, wqr = _split_w_uq(bf(w_uq[l]))
    q, k, vt, pool, gate = _mixer_in(
        x1, cos, sin, row(mix_pre_g[l]), w_in, row(q_a_norm_g[l]),
        wq, wqr, row(kv_a_norm_g[l]), bf(w_uk[l]), bf(w_uv[l]).T)
    o = _attention(q, k, vt.reshape(b, N_HEADS, V_HEAD_DIM, s))
    x2, *ffn2_w = _mixer_out(o, pool, gate, x1, bf(w_o_attn[l]), bf(pool_w[l]), row(pool_scale[l]),
                             bf(w_o_pool[l]), bf(w_out[l]), row(mix_post_g[l]),
                             [ffn2_w_gate[l], ffn2_w_up[l], ffn2_w_down[l]])
    return _ffn(x2.reshape(b * s, d), row(ffn2_pre_g[l]), *ffn2_w, row(ffn2_post_g[l]), row(final_g[l]),
                True).reshape(b, s, d)
```

```python
import functools
import math

import jax
import jax.numpy as jnp
from jax.experimental import pallas as pl
from jax.experimental.pallas import tpu as pltpu

D_MODEL = 1024
N_HEADS = 8
QK_NOPE_DIM = 128
QK_ROPE_DIM = 64
QK_DIM = QK_NOPE_DIM + QK_ROPE_DIM
V_HEAD_DIM = 128
Q_LORA_RANK = 384
KV_LORA_RANK = 256
ROPE_THETA = 10000.0
POOL_WINDOWS = (2, 4, 8, 16)
POOL_GROUP = 128
POOL_WIDTH = POOL_GROUP * len(POOL_WINDOWS)
D_FF = 2816
MACARON_WEIGHT = 0.5
NORM_EPS = 1e-6

LANES = 128
POOL_HALO = 8

OFF_CQ = 0
OFF_CKV = OFF_CQ + Q_LORA_RANK
OFF_KR = OFF_CKV + KV_LORA_RANK
OFF_POOL = OFF_KR + 2 * QK_ROPE_DIM
OFF_GATE = OFF_POOL + POOL_WIDTH
IN_WIDTH_EXT = OFF_GATE + 2 * D_MODEL
W_IN_ROWS = 64
FFN_W_STEPS = 11
ROPE_TABLE_ROWS = 512

FFN_TM = 1024
FFN_SUB = 256
MIX_IN_TM = 512
MIX_OUT_TM = 1024
MIX_SUB = 256
ATT_TQ = 512
ATT_KC = 256
ATT_HEADS = 4
VMEM_LIMIT = 56 * 1024 * 1024


def _resident(shape):
    return pl.BlockSpec(shape, lambda *_: (0,) * len(shape), pipeline_mode=pl.Buffered(1))


def _rmsnorm(x, g):
    ms = jnp.mean(x * x, axis=-1, keepdims=True)
    return x * jax.lax.rsqrt(ms + NORM_EPS) * g


def _bdot(a, b):
    return jnp.dot(a.astype(jnp.bfloat16), b, preferred_element_type=jnp.float32)


def _rope_table_block(pos_ref, invf_ref, cos_ref, sin_ref):
    ang = pos_ref[...].astype(jnp.float32) * invf_ref[...]
    n_freq = QK_ROPE_DIM // 2
    tok_per_row = LANES // n_freq
    rows = ang.shape[0]
    for table, out_ref in ((jnp.cos(ang), cos_ref), (jnp.sin(ang), sin_ref)):
        for k in range(tok_per_row):
            piece = table[:, k * n_freq:(k + 1) * n_freq]
            out_ref[pl.ds(k, rows, stride=tok_per_row), :] = jnp.concatenate([piece] * tok_per_row, axis=1)


def _ffn_kernel(x_ref, pre_g_ref, wg_ref, wu_ref, wd_ref, post_g_ref, final_g_ref, *rest, final_norm, w_steps,
                rope_steps):
    if rope_steps:
        pos_ref, invf_ref, o_ref, cos_ref, sin_ref, *w_scratch = rest
    else:
        o_ref, *w_scratch = rest
    weights = w_scratch if w_steps else (wg_ref, wu_ref, wd_ref)

    def row_tile():
        for r in range(x_ref.shape[0] // FFN_SUB):
            rows = slice(r * FFN_SUB, (r + 1) * FFN_SUB)
            x = x_ref[rows, :]
            xn = _rmsnorm(x, pre_g_ref[...]).astype(jnp.bfloat16)
            g = jnp.dot(xn, weights[0][...], preferred_element_type=jnp.float32)
            u = jnp.dot(xn, weights[1][...], preferred_element_type=jnp.float32)
            h = (0.5 * g) * (jnp.tanh(0.5 * g) + 1.0) * u
            f = _bdot(h, weights[2][...])
            y = x + MACARON_WEIGHT * _rmsnorm(f, post_g_ref[...])
            if final_norm:
                y = _rmsnorm(y, final_g_ref[...])
            o_ref[rows, :] = y

    if not w_steps:
        row_tile()
        return
    i = pl.program_id(0)
    fc = wg_ref.shape[1]
    for c in range(w_steps):
        @pl.when(i == c)
        def _(c=c):
            cols = slice(c * fc, (c + 1) * fc)
            weights[0][:, cols] = wg_ref[...].astype(jnp.bfloat16)
            weights[1][:, cols] = wu_ref[...].astype(jnp.bfloat16)
            weights[2][cols, :] = wd_ref[...].astype(jnp.bfloat16)
    if rope_steps:
        pl.when(i < rope_steps)(lambda: _rope_table_block(pos_ref, invf_ref, cos_ref, sin_ref))
    pl.when(i >= w_steps)(row_tile)


def _ffn(x2d, pre_g, wg, wu, wd, post_g, final_g, final_norm, rope=None):
    n, d = x2d.shape
    f = wg.shape[1]
    tm = FFN_TM
    w_steps = 0 if wg.dtype == jnp.bfloat16 else FFN_W_STEPS
    rope_steps, rope_in, rope_specs, rope_out_specs, rope_shapes = 0, [], [], [], []
    if rope is not None:
        pos_rep, invf = rope
        tr = ROPE_TABLE_ROWS
        tok_per_row = LANES // (QK_ROPE_DIM // 2)
        rope_steps = pos_rep.shape[0] // tr
        assert 0 < rope_steps <= w_steps
        blk = lambda i: (jnp.minimum(i, rope_steps - 1), 0)
        rope_in = [pos_rep, invf]
        rope_specs = [pl.BlockSpec((tr, LANES), blk), _resident(invf.shape)]
        rope_out_specs = [pl.BlockSpec((tr * tok_per_row, LANES), blk)] * 2
        rope_shapes = [jax.ShapeDtypeStruct((pos_rep.shape[0] * tok_per_row, LANES), jnp.float32)] * 2
    if w_steps:
        fc = f // w_steps
        chunk = lambda i: jnp.minimum(i, w_steps - 1)
        w_specs = [pl.BlockSpec((d, fc), lambda i: (0, chunk(i))), pl.BlockSpec((d, fc), lambda i: (0, chunk(i))),
                   pl.BlockSpec((fc, d), lambda i: (chunk(i), 0))]
        w_scratch = [pltpu.VMEM((d, f), jnp.bfloat16), pltpu.VMEM((d, f), jnp.bfloat16),
                     pltpu.VMEM((f, d), jnp.bfloat16)]
    else:
        w_specs, w_scratch = [_resident(wg.shape), _resident(wu.shape), _resident(wd.shape)], []
    row = pl.BlockSpec((tm, d), lambda i: (jnp.maximum(i - w_steps, 0), 0))
    outs = pl.pallas_call(
        functools.partial(_ffn_kernel, final_norm=final_norm, w_steps=w_steps, rope_steps=rope_steps),
        grid=(w_steps + n // tm,),
        in_specs=[row, _resident((1, d))] + w_specs + [_resident((1, d)), _resident((1, d))] + rope_specs,
        out_specs=[row] + rope_out_specs,
        out_shape=[jax.ShapeDtypeStruct((n, d), jnp.float32)] + rope_shapes,
        scratch_shapes=w_scratch,
        compiler_params=pltpu.CompilerParams(dimension_semantics=("arbitrary",), vmem_limit_bytes=VMEM_LIMIT),
        name="ffn_final" if final_norm else "ffn",
    )(x2d, pre_g, wg, wu, wd, post_g, final_g, *rope_in)
    return outs if rope_steps else outs[0]


def _cast_stream_specs(weights, n_steps, step_of):
    spec = lambda w: pl.BlockSpec((w.shape[0] // n_steps, w.shape[1]), lambda *g: (step_of(*g), 0))
    specs = [spec(w) for w in weights]
    return specs, specs, [jax.ShapeDtypeStruct(w.shape, jnp.bfloat16) for w in weights]


def _cast_stream_step(in_refs, out_refs):
    for src, dst in zip(in_refs, out_refs):
        dst[...] = src[...].astype(jnp.bfloat16)


def _extend_w_in(w_ref, wext_ref):
    kr1 = OFF_KR + QK_ROPE_DIM
    half = QK_ROPE_DIM // 2
    n_in = w_ref.shape[1]

    def body(rb, carry):
        rows = pl.ds(pl.multiple_of(rb * W_IN_ROWS, W_IN_ROWS), W_IN_ROWS)
        wext_ref[rows, 0:kr1] = w_ref[rows, 0:kr1].astype(jnp.bfloat16)
        wext_ref[rows, kr1:kr1 + half] = w_ref[rows, OFF_KR + half:kr1].astype(jnp.bfloat16)
        wext_ref[rows, kr1 + half:OFF_POOL] = w_ref[rows, OFF_KR:OFF_KR + half].astype(jnp.bfloat16)
        wext_ref[rows, OFF_POOL:IN_WIDTH_EXT] = w_ref[rows, kr1:n_in].astype(jnp.bfloat16)
        return carry

    jax.lax.fori_loop(0, w_ref.shape[0] // W_IN_ROWS, body, 0)


def _mixer_in_kernel(x_ref, cos_ref, sin_ref, pre_g_ref, w_in_ref, *refs):
    weights, (q_ref, k_ref, vt_ref, pool_ref, gate_ref), wext_ref = refs[:-6], refs[-6:-1], refs[-1]

    @pl.when((pl.program_id(0) == 0) & (pl.program_id(1) == 0))
    def _():
        _extend_w_in(w_in_ref, wext_ref)

    weights = (pre_g_ref, wext_ref) + tuple(weights)
    for r in range(x_ref.shape[1] // MIX_SUB):
        rows = pl.ds(r * MIX_SUB, MIX_SUB)
        _mixer_in_tile(x_ref.at[:, rows, :], cos_ref.at[:, rows, :], sin_ref.at[:, rows, :], *weights,
                       q_ref.at[:, :, rows, :], k_ref.at[:, :, rows, :], vt_ref.at[:, :, rows],
                       pool_ref.at[:, rows, :], gate_ref.at[:, rows, :])


def _mixer_in_tile(x_ref, cos_ref, sin_ref, pre_g_ref, w_in_ref, qg_ref, wq_ref, wqr_ref, kvg_ref, wuk_ref, wuvt_ref,
                   q_ref, k_ref, vt_ref, pool_ref, gate_ref):
    u = _rmsnorm(x_ref[0], pre_g_ref[...]).astype(jnp.bfloat16)
    z = jnp.dot(u, w_in_ref[:, :OFF_GATE], preferred_element_type=jnp.float32)
    pool_ref[0] = z[:, OFF_POOL:OFF_POOL + POOL_WIDTH]

    gl = jnp.dot(u, w_in_ref[:, OFF_GATE:], preferred_element_type=jnp.float32)
    gate_ref[0] = (0.5 * jnp.tanh(0.5 * gl) + 0.5).astype(jnp.bfloat16)

    cos = cos_ref[0]
    sin = sin_ref[0]
    lane = jax.lax.broadcasted_iota(jnp.int32, cos.shape, 1)
    first_half = (lane % QK_ROPE_DIM) < (QK_ROPE_DIM // 2)
    ssin = jnp.where(first_half, -sin, sin)

    kr2 = z[:, OFF_KR:OFF_KR + LANES] * jnp.where(lane < QK_ROPE_DIM, cos, ssin)
    k_rope = (kr2[:, :QK_ROPE_DIM] + kr2[:, QK_ROPE_DIM:]).astype(jnp.bfloat16)
    ckv = _rmsnorm(z[:, OFF_CKV:OFF_CKV + KV_LORA_RANK], kvg_ref[...]).astype(jnp.bfloat16)
    cq = _rmsnorm(z[:, OFF_CQ:OFF_CQ + Q_LORA_RANK], qg_ref[...]).astype(jnp.bfloat16)

    k_nope = jnp.dot(ckv, wuk_ref[...], preferred_element_type=jnp.float32).astype(jnp.bfloat16)
    for h in range(N_HEADS):
        k_ref[0, h, :, 0:QK_NOPE_DIM] = k_nope[:, h * QK_NOPE_DIM:(h + 1) * QK_NOPE_DIM]
        k_ref[0, h, :, QK_NOPE_DIM:QK_DIM] = k_rope
    vt_ref[0] = jax.lax.dot_general(wuvt_ref[...], ckv, (((1,), (1,)), ((), ())),
                                    preferred_element_type=jnp.float32).astype(jnp.bfloat16)

    scale = math.log2(math.e) / math.sqrt(QK_DIM)
    qr = jnp.dot(cq, wqr_ref[...], preferred_element_type=jnp.float32)
    n_rep = N_HEADS * QK_ROPE_DIM // LANES
    cos_h = jnp.concatenate([cos] * n_rep, axis=1)
    ssin_h = jnp.concatenate([ssin] * n_rep, axis=1)
    half = QK_ROPE_DIM // 2
    qs = jnp.where(jnp.concatenate([first_half] * n_rep, axis=1),
                   pltpu.roll(qr, qr.shape[1] - half, axis=1), pltpu.roll(qr, half, axis=1))
    q_rope = ((qr * cos_h + qs * ssin_h) * scale).astype(jnp.bfloat16)
    q_nope = (jnp.dot(cq, wq_ref[...], preferred_element_type=jnp.float32) * scale).astype(jnp.bfloat16)
    for h in range(N_HEADS):
        q_ref[0, h, :, 0:QK_NOPE_DIM] = q_nope[:, h * QK_NOPE_DIM:(h + 1) * QK_NOPE_DIM]
        q_ref[0, h, :, QK_NOPE_DIM:QK_DIM] = q_rope[:, h * QK_ROPE_DIM:(h + 1) * QK_ROPE_DIM]


def _mixer_in(x, cos, sin, pre_g, w_in, qg, wq, wqr, kvg, wuk, wuv):
    b, s, d = x.shape
    tm = MIX_IN_TM
    heads = lambda w: pl.BlockSpec((1, N_HEADS, tm, w), lambda bi, i: (bi, 0, i, 0))
    rows = lambda w: pl.BlockSpec((1, tm, w), lambda bi, i: (bi, i, 0))
    return pl.pallas_call(
        _mixer_in_kernel,
        grid=(b, s // tm),
        in_specs=[rows(d), rows(LANES), rows(LANES), _resident((1, d)),
                  pl.BlockSpec((None,) + w_in.shape[1:], lambda *_: (0, 0, 0), pipeline_mode=pl.Buffered(1)),
                  _resident(qg.shape),
                  _resident(wq.shape), _resident(wqr.shape), _resident(kvg.shape),
                  _resident(wuk.shape), _resident(wuv.shape)],
        out_specs=[heads(QK_DIM), heads(QK_DIM), pl.BlockSpec((1, N_HEADS * V_HEAD_DIM, tm), lambda bi, i: (bi, 0, i)), rows(POOL_WIDTH), rows(2 * D_MODEL)],
        out_shape=[jax.ShapeDtypeStruct((b, N_HEADS, s, QK_DIM), jnp.bfloat16),
                   jax.ShapeDtypeStruct((b, N_HEADS, s, QK_DIM), jnp.bfloat16),
                   jax.ShapeDtypeStruct((b, N_HEADS * V_HEAD_DIM, s), jnp.bfloat16),
                   jax.ShapeDtypeStruct((b, s, POOL_WIDTH), jnp.float32),
                   jax.ShapeDtypeStruct((b, s, 2 * D_MODEL), jnp.bfloat16)],
        scratch_shapes=[pltpu.VMEM((d, IN_WIDTH_EXT), jnp.bfloat16)],
        compiler_params=pltpu.CompilerParams(dimension_semantics=("arbitrary", "arbitrary"),
                                             vmem_limit_bytes=VMEM_LIMIT),
        name="mixer_in",
    )(x, cos, sin, pre_g, w_in, qg, wq, wqr, kvg, wuk, wuv)


def _attn_kernel(q_ref, k_ref, vt_ref, o_ref, st_ref, p_ref):
    n_heads, s_len = k_ref.shape[1], k_ref.shape[2]
    n_sub = s_len // ATT_TQ
    n_chunk = s_len // ATT_KC
    tiles = [(hh, j) for hh in range(n_heads) for j in range(n_sub)]

    def logits(t):
        hh, j = tiles[t]
        st = jax.lax.dot_general(k_ref[0, hh], q_ref[0, hh, j * ATT_TQ:(j + 1) * ATT_TQ, :],
                                 (((1,), (1,)), ((), ())), preferred_element_type=jnp.float32)
        st_ref[t % 2] = st
        return jnp.max(st, axis=0, keepdims=True)

    m_next = logits(0)
    for t, (hh, j) in enumerate(tiles):
        slot = t % 2
        m = m_next
        if t + 1 < len(tiles):
            m_next = logits(t + 1)
        l = jnp.zeros_like(m)
        for c in range(n_chunk):
            keys = slice(c * ATT_KC, (c + 1) * ATT_KC)
            p = jnp.exp2(st_ref[slot, keys, :] - m)
            l = l + jnp.sum(p, axis=0, keepdims=True)
            p_ref[slot, keys, :] = p.astype(jnp.bfloat16)
        ot = jnp.dot(vt_ref[0, hh], p_ref[slot], preferred_element_type=jnp.float32)
        o_ref[0, j * ATT_TQ:(j + 1) * ATT_TQ, hh * V_HEAD_DIM:(hh + 1) * V_HEAD_DIM] = (
            (ot / l).T.astype(jnp.bfloat16))


def _attention(q, k, vt):
    b, h, s, _ = q.shape
    hs = ATT_HEADS
    head = lambda r, c: pl.BlockSpec((1, hs, r, c), lambda bi, hi: (bi, hi, 0, 0))
    return pl.pallas_call(
        _attn_kernel,
        grid=(b, h // hs),
        in_specs=[head(s, QK_DIM), head(s, QK_DIM), head(V_HEAD_DIM, s)],
        out_specs=pl.BlockSpec((1, s, hs * V_HEAD_DIM), lambda bi, hi: (bi, 0, hi)),
        out_shape=jax.ShapeDtypeStruct((b, s, h * V_HEAD_DIM), jnp.bfloat16),
        scratch_shapes=[pltpu.VMEM((2, s, ATT_TQ), jnp.float32), pltpu.VMEM((2, s, ATT_TQ), jnp.bfloat16)],
        compiler_params=pltpu.CompilerParams(dimension_semantics=("arbitrary", "arbitrary"),
                                             vmem_limit_bytes=VMEM_LIMIT),
        name="attention",
    )(q, k, vt)


def _mixer_out_kernel(o_ref, pool_ref, prev_ref, next_ref, gate_ref, x_ref, wo_ref, pw_ref, ps_ref, wop_ref,
                      wout_ref, post_g_ref, *rest, seq_len):
    n_w = (len(rest) - 2) // 2
    out_ref, ext_ref = rest[n_w], rest[-1]
    _cast_stream_step(rest[:n_w], rest[n_w + 1:-1])
    i = pl.program_id(1)
    n_i = pl.num_programs(1)
    tm = pool_ref.shape[1]

    ext_ref[0:POOL_HALO, :] = jnp.where(i > 0, prev_ref[0], 0.0)
    ext_ref[POOL_HALO:POOL_HALO + tm, :] = pool_ref[0]
    ext_ref[POOL_HALO + tm:POOL_HALO + tm + POOL_HALO, :] = jnp.where(i < n_i - 1, next_ref[0], 0.0)

    win = MIX_SUB + 2 * POOL_HALO
    for r in range(tm // MIX_SUB):
        r0 = r * MIX_SUB
        rows = slice(r0, r0 + MIX_SUB)
        o_rows = o_ref[0, rows, :]
        t_first = i * tm + r0 + jax.lax.broadcasted_iota(jnp.int32, (POOL_HALO, 1), 0)
        t_last = t_first + (MIX_SUB - POOL_HALO)
        ys, y_attn_parts = [], []
        n_col = D_MODEL // len(POOL_WINDOWS)
        for gi, w in enumerate(POOL_WINDOWS):
            y_attn_parts.append(jnp.dot(o_rows, wo_ref[:, gi * n_col:(gi + 1) * n_col],
                                        preferred_element_type=jnp.float32))
            left = w // 2
            right = w - 1 - left
            cols = slice(gi * POOL_GROUP, (gi + 1) * POOL_GROUP)
            xe = ext_ref[r0:r0 + win, cols]
            fwd, span = xe, 1
            while span < min(w, POOL_HALO):
                fwd = fwd + pltpu.roll(fwd, win - span, axis=0)
                span *= 2
            if w > span:
                fwd = fwd + pltpu.roll(fwd, win - span, axis=0)
            total = pltpu.roll(fwd, left, axis=0)[POOL_HALO:POOL_HALO + MIX_SUB]
            inv = [1.0 / (jnp.minimum(tt + right + 1, seq_len) - jnp.maximum(tt - left, 0)).astype(jnp.float32)
                   for tt in (t_first, t_last)]
            mean = jnp.concatenate([total[:POOL_HALO] * inv[0], total[POOL_HALO:-POOL_HALO] * (1.0 / w),
                                    total[-POOL_HALO:] * inv[1]], axis=0)
            dg = mean - xe[POOL_HALO:POOL_HALO + MIX_SUB]
            ys.append(_bdot(dg, pw_ref[gi]))
        y = jnp.concatenate(ys, axis=1) * ps_ref[...]
        y_pool = _bdot(y, wop_ref[...])
        y_attn = jnp.concatenate(y_attn_parts, axis=1)

        g_attn = gate_ref[0, rows, 0:D_MODEL].astype(jnp.float32)
        g_pool = gate_ref[0, rows, D_MODEL:2 * D_MODEL].astype(jnp.float32)
        mixed = _bdot(g_attn * y_attn + g_pool * y_pool, wout_ref[...])
        out_ref[0, rows, :] = x_ref[0, rows, :] + _rmsnorm(mixed, post_g_ref[...])


def _mixer_out(o, pool, gate, x, wo, pw, ps, wop, wout, post_g, weights):
    b, s, d = x.shape
    tm = MIX_OUT_TM
    nb = tm // POOL_HALO
    last = s // POOL_HALO - 1
    n_i = s // tm
    w_in_specs, w_out_specs, w_shapes = _cast_stream_specs(weights, b * n_i, lambda bi, i: bi * n_i + i)
    rows = lambda w: pl.BlockSpec((1, tm, w), lambda bi, i: (bi, i, 0))
    prev = pl.BlockSpec((1, POOL_HALO, POOL_WIDTH), lambda bi, i: (bi, jnp.maximum(i * nb - 1, 0), 0))
    nxt = pl.BlockSpec((1, POOL_HALO, POOL_WIDTH), lambda bi, i: (bi, jnp.minimum((i + 1) * nb, last), 0))
    return pl.pallas_call(
        functools.partial(_mixer_out_kernel, seq_len=s),
        grid=(b, s // tm),
        in_specs=[rows(d), rows(POOL_WIDTH), prev, nxt, rows(2 * d), rows(d), _resident(wo.shape),
                  _resident(pw.shape), _resident(ps.shape), _resident(wop.shape), _resident(wout.shape),
                  _resident(post_g.shape)] + w_in_specs,
        out_specs=[rows(d)] + w_out_specs,
        out_shape=[jax.ShapeDtypeStruct((b, s, d), jnp.float32)] + w_shapes,
        scratch_shapes=[pltpu.VMEM((tm + 2 * POOL_HALO, POOL_WIDTH), jnp.float32)],
        compiler_params=pltpu.CompilerParams(dimension_semantics=("arbitrary", "arbitrary"),
                                             vmem_limit_bytes=VMEM_LIMIT),
        name="mixer_out",
    )(o, pool, pool, pool, gate, x, wo, pw, ps, wop, wout, post_g, *weights)


def _split_w_uq(w_uq):
    r = w_uq.shape[0]
    w3 = w_uq.reshape(r, N_HEADS, QK_DIM)
    nope = w3[:, :, :QK_NOPE_DIM].reshape(r, N_HEADS * QK_NOPE_DIM)
    return nope, w3[:, :, QK_NOPE_DIM:].reshape(r, N_HEADS * QK_ROPE_DIM)


def kernel(x, positions, ffn1_pre_g, ffn1_w_gate, ffn1_w_up, ffn1_w_down, ffn1_post_g, mix_pre_g, w_in, q_a_norm_g, w_uq, kv_a_norm_g, w_uk, w_uv, w_o_attn, pool_w, pool_scale, w_o_pool, w_out, mix_post_g, ffn2_pre_g, ffn2_w_gate, ffn2_w_up, ffn2_w_down, ffn2_post_g, final_g):
    b, s, d = x.shape
    bf = lambda w: w.astype(jnp.bfloat16)
    row = lambda g: g.reshape(1, -1)
    inv_freq = ROPE_THETA ** (-jnp.arange(0, QK_ROPE_DIM, 2, dtype=jnp.float32) / QK_ROPE_DIM)
    n_freq = QK_ROPE_DIM // 2
    invf = jnp.tile(inv_freq, LANES // n_freq).reshape(1, LANES)
    pos_rep = jnp.repeat(positions.reshape(b * s * n_freq // LANES, LANES // n_freq), n_freq, axis=1)
    assert w_in.shape[0] == 1, "one layer: the second FFN's weight casts are hosted by the mixer_out before it"
    l = 0

    x1, cos, sin = _ffn(x.reshape(b * s, d), row(ffn1_pre_g[l]), ffn1_w_gate[l], ffn1_w_up[l], ffn1_w_down[l],
                        row(ffn1_post_g[l]), row(final_g[l]), False, rope=(pos_rep, invf))
    x1, cos, sin = x1.reshape(b, s, d), cos.reshape(b, s, LANES), sin.reshape(b, s, LANES)
    wq, wqr = _split_w_uq(bf(w_uq[l]))
    q, k, vt, pool, gate = _mixer_in(
        x1, cos, sin, row(mix_pre_g[l]), w_in, row(q_a_norm_g[l]),
        wq, wqr, row(kv_a_norm_g[l]), bf(w_uk[l]), bf(w_uv[l]).T)
    o = _attention(q, k, vt.reshape(b, N_HEADS, V_HEAD_DIM, s))
    x2, *ffn2_w = _mixer_out(o, pool, gate, x1, bf(w_o_attn[l]), bf(pool_w[l]), row(pool_scale[l]),
                             bf(w_o_pool[l]), bf(w_out[l]), row(mix_post_g[l]),
                             [ffn2_w_gate[l], ffn2_w_up[l], ffn2_w_down[l]])
    return _ffn(x2.reshape(b * s, d), row(ffn2_pre_g[l]), *ffn2_w, row(ffn2_post_g[l]), row(final_g[l]),
                True).reshape(b, s, d)
```

```python
import functools
import math

import jax
import jax.numpy as jnp
from jax.experimental import pallas as pl
from jax.experimental.pallas import tpu as pltpu

D_MODEL = 1024
N_HEADS = 8
QK_NOPE_DIM = 128
QK_ROPE_DIM = 64
QK_DIM = QK_NOPE_DIM + QK_ROPE_DIM
V_HEAD_DIM = 128
Q_LORA_RANK = 384
KV_LORA_RANK = 256
ROPE_THETA = 10000.0
POOL_WINDOWS = (2, 4, 8, 16)
POOL_GROUP = 128
POOL_WIDTH = POOL_GROUP * len(POOL_WINDOWS)
D_FF = 2816
MACARON_WEIGHT = 0.5
NORM_EPS = 1e-6

LANES = 128
POOL_HALO = 8

OFF_CQ = 0
OFF_CKV = OFF_CQ + Q_LORA_RANK
OFF_KR = OFF_CKV + KV_LORA_RANK
OFF_POOL = OFF_KR + 2 * QK_ROPE_DIM
OFF_GATE = OFF_POOL + POOL_WIDTH
IN_WIDTH_EXT = OFF_GATE + 2 * D_MODEL
W_IN_ROWS = 64
FFN_W_STEPS = 11
ROPE_TABLE_ROWS = 512

FFN_TM = 1024
FFN_SUB = 256
MIX_IN_TM = 512
MIX_OUT_TM = 1024
MIX_SUB = 256
ATT_TQ = 512
ATT_KC = 256
ATT_HEADS = 4
VMEM_LIMIT = 56 * 1024 * 1024


def _resident(shape):
    return pl.BlockSpec(shape, lambda *_: (0,) * len(shape), pipeline_mode=pl.Buffered(1))


def _rmsnorm(x, g):
    ms = jnp.mean(x * x, axis=-1, keepdims=True)
    return x * jax.lax.rsqrt(ms + NORM_EPS) * g


def _bdot(a, b):
    return jnp.dot(a.astype(jnp.bfloat16), b, preferred_element_type=jnp.float32)


def _rope_table_block(pos_ref, invf_ref, cos_ref, sin_ref):
    ang = pos_ref[...].astype(jnp.float32) * invf_ref[...]
    n_freq = QK_ROPE_DIM // 2
    tok_per_row = LANES // n_freq
    rows = ang.shape[0]
    for table, out_ref in ((jnp.cos(ang), cos_ref), (jnp.sin(ang), sin_ref)):
        for k in range(tok_per_row):
            piece = table[:, k * n_freq:(k + 1) * n_freq]
            out_ref[pl.ds(k, rows, stride=tok_per_row), :] = jnp.concatenate([piece] * tok_per_row, axis=1)


def _ffn_kernel(x_ref, pre_g_ref, wg_ref, wu_ref, wd_ref, post_g_ref, final_g_ref, *rest, final_norm, w_steps,
                rope_steps, n_cast):
    rest = list(rest)
    pos_ref, invf_ref = (rest.pop(0), rest.pop(0)) if rope_steps else (None, None)
    cast_in = [rest.pop(0) for _ in range(n_cast)]
    o_ref = rest.pop(0)
    cos_ref, sin_ref = (rest.pop(0), rest.pop(0)) if rope_steps else (None, None)
    cast_out = [rest.pop(0) for _ in range(n_cast)]
    w_scratch = rest
    weights = w_scratch if w_steps else (wg_ref, wu_ref, wd_ref)

    def row_tile():
        _cast_stream_step(cast_in, cast_out)
        for r in range(x_ref.shape[0] // FFN_SUB):
            rows = slice(r * FFN_SUB, (r + 1) * FFN_SUB)
            x = x_ref[rows, :]
            xn = _rmsnorm(x, pre_g_ref[...]).astype(jnp.bfloat16)
            g = jnp.dot(xn, weights[0][...], preferred_element_type=jnp.float32)
            u = jnp.dot(xn, weights[1][...], preferred_element_type=jnp.float32)
            h = g * (1.0 / (1.0 + jnp.exp(-g))) * u
            f = _bdot(h, weights[2][...])
            y = x + MACARON_WEIGHT * _rmsnorm(f, post_g_ref[...])
            if final_norm:
                y = _rmsnorm(y, final_g_ref[...])
            o_ref[rows, :] = y

    if not w_steps:
        row_tile()
        return
    i = pl.program_id(0)
    fc = wg_ref.shape[1]
    for c in range(w_steps):
        @pl.when(i == c)
        def _(c=c):
            cols = slice(c * fc, (c + 1) * fc)
            weights[0][:, cols] = wg_ref[...].astype(jnp.bfloat16)
            weights[1][:, cols] = wu_ref[...].astype(jnp.bfloat16)
            weights[2][cols, :] = wd_ref[...].astype(jnp.bfloat16)
    if rope_steps:
        pl.when(i < rope_steps)(lambda: _rope_table_block(pos_ref, invf_ref, cos_ref, sin_ref))
    pl.when(i >= w_steps)(row_tile)


def _ffn(x2d, pre_g, wg, wu, wd, post_g, final_g, final_norm, rope=None, cast_weights=()):
    cast_weights = list(cast_weights)
    n, d = x2d.shape
    f = wg.shape[1]
    tm = FFN_TM
    w_steps = 0 if wg.dtype == jnp.bfloat16 else FFN_W_STEPS
    rope_steps, rope_in, rope_specs, rope_out_specs, rope_shapes = 0, [], [], [], []
    if rope is not None:
        pos_rep, invf = rope
        tr = ROPE_TABLE_ROWS
        tok_per_row = LANES // (QK_ROPE_DIM // 2)
        rope_steps = pos_rep.shape[0] // tr
        assert 0 < rope_steps <= w_steps
        blk = lambda i: (jnp.minimum(i, rope_steps - 1), 0)
        rope_in = [pos_rep, invf]
        rope_specs = [pl.BlockSpec((tr, LANES), blk), _resident(invf.shape)]
        rope_out_specs = [pl.BlockSpec((tr * tok_per_row, LANES), blk)] * 2
        rope_shapes = [jax.ShapeDtypeStruct((pos_rep.shape[0] * tok_per_row, LANES), jnp.float32)] * 2
    if w_steps:
        fc = f // w_steps
        chunk = lambda i: jnp.minimum(i, w_steps - 1)
        w_specs = [pl.BlockSpec((d, fc), lambda i: (0, chunk(i))), pl.BlockSpec((d, fc), lambda i: (0, chunk(i))),
                   pl.BlockSpec((fc, d), lambda i: (chunk(i), 0))]
        w_scratch = [pltpu.VMEM((d, f), jnp.bfloat16), pltpu.VMEM((d, f), jnp.bfloat16),
                     pltpu.VMEM((f, d), jnp.bfloat16)]
    else:
        w_specs, w_scratch = [_resident(wg.shape), _resident(wu.shape), _resident(wd.shape)], []
    tile = lambda i: jnp.maximum(i - w_steps, 0)
    row = pl.BlockSpec((tm, d), lambda i: (tile(i), 0))
    cast_in_specs, cast_out_specs, cast_shapes = _cast_stream_specs(cast_weights, n // tm, tile)
    outs = pl.pallas_call(
        functools.partial(_ffn_kernel, final_norm=final_norm, w_steps=w_steps, rope_steps=rope_steps,
                          n_cast=len(cast_weights)),
        grid=(w_steps + n // tm,),
        in_specs=([row, _resident((1, d))] + w_specs + [_resident((1, d)), _resident((1, d))] + rope_specs
                  + cast_in_specs),
        out_specs=[row] + rope_out_specs + cast_out_specs,
        out_shape=[jax.ShapeDtypeStruct((n, d), jnp.float32)] + rope_shapes + cast_shapes,
        scratch_shapes=w_scratch,
        compiler_params=pltpu.CompilerParams(dimension_semantics=("arbitrary",), vmem_limit_bytes=VMEM_LIMIT),
        name="ffn_final" if final_norm else "ffn",
    )(x2d, pre_g, wg, wu, wd, post_g, final_g, *rope_in, *cast_weights)
    return outs if len(outs) > 1 else outs[0]


def _cast_stream_specs(weights, n_steps, step_of):
    spec = lambda w: pl.BlockSpec((w.shape[0] // n_steps, w.shape[1]), lambda *g: (step_of(*g), 0))
    specs = [spec(w) for w in weights]
    return specs, specs, [jax.ShapeDtypeStruct(w.shape, jnp.bfloat16) for w in weights]


def _cast_stream_step(in_refs, out_refs):
    for src, dst in zip(in_refs, out_refs):
        dst[...] = src[...].astype(jnp.bfloat16)


def _extend_w_in(w_ref, wext_ref):
    kr1 = OFF_KR + QK_ROPE_DIM
    half = QK_ROPE_DIM // 2
    n_in = w_ref.shape[1]

    def body(rb, carry):
        rows = pl.ds(pl.multiple_of(rb * W_IN_ROWS, W_IN_ROWS), W_IN_ROWS)
        wext_ref[rows, 0:kr1] = w_ref[rows, 0:kr1].astype(jnp.bfloat16)
        wext_ref[rows, kr1:kr1 + half] = w_ref[rows, OFF_KR + half:kr1].astype(jnp.bfloat16)
        wext_ref[rows, kr1 + half:OFF_POOL] = w_ref[rows, OFF_KR:OFF_KR + half].astype(jnp.bfloat16)
        wext_ref[rows, OFF_POOL:IN_WIDTH_EXT] = w_ref[rows, kr1:n_in].astype(jnp.bfloat16)
        return carry

    jax.lax.fori_loop(0, w_ref.shape[0] // W_IN_ROWS, body, 0)


def _mixer_in_kernel(x_ref, cos_ref, sin_ref, pre_g_ref, w_in_ref, *refs):
    weights, (q_ref, k_ref, vt_ref, pool_ref, gate_ref), wext_ref = refs[:-6], refs[-6:-1], refs[-1]

    @pl.when((pl.program_id(0) == 0) & (pl.program_id(1) == 0))
    def _():
        _extend_w_in(w_in_ref, wext_ref)

    weights = (pre_g_ref, wext_ref) + tuple(weights)
    for r in range(x_ref.shape[1] // MIX_SUB):
        rows = pl.ds(r * MIX_SUB, MIX_SUB)
        _mixer_in_tile(x_ref.at[:, rows, :], cos_ref.at[:, rows, :], sin_ref.at[:, rows, :], *weights,
                       q_ref.at[:, :, rows, :], k_ref.at[:, :, rows, :], vt_ref.at[:, :, rows],
                       pool_ref.at[:, rows, :], gate_ref.at[:, rows, :])


def _mixer_in_tile(x_ref, cos_ref, sin_ref, pre_g_ref, w_in_ref, qg_ref, wq_ref, wqr_ref, kvg_ref, wuk_ref, wuvt_ref,
                   q_ref, k_ref, vt_ref, pool_ref, gate_ref):
    u = _rmsnorm(x_ref[0], pre_g_ref[...]).astype(jnp.bfloat16)
    z = jnp.dot(u, w_in_ref[:, :OFF_GATE], preferred_element_type=jnp.float32)
    pool_ref[0] = z[:, OFF_POOL:OFF_POOL + POOL_WIDTH]

    gl = jnp.dot(u, w_in_ref[:, OFF_GATE:], preferred_element_type=jnp.float32)
    gate_ref[0] = (1.0 / (1.0 + jnp.exp(-gl))).astype(jnp.bfloat16)

    cos = cos_ref[0]
    sin = sin_ref[0]
    lane = jax.lax.broadcasted_iota(jnp.int32, cos.shape, 1)
    first_half = (lane % QK_ROPE_DIM) < (QK_ROPE_DIM // 2)
    ssin = jnp.where(first_half, -sin, sin)

    kr2 = z[:, OFF_KR:OFF_KR + LANES] * jnp.where(lane < QK_ROPE_DIM, cos, ssin)
    k_rope = (kr2[:, :QK_ROPE_DIM] + kr2[:, QK_ROPE_DIM:]).astype(jnp.bfloat16)
    ckv = _rmsnorm(z[:, OFF_CKV:OFF_CKV + KV_LORA_RANK], kvg_ref[...]).astype(jnp.bfloat16)
    cq = _rmsnorm(z[:, OFF_CQ:OFF_CQ + Q_LORA_RANK], qg_ref[...]).astype(jnp.bfloat16)

    k_nope = jnp.dot(ckv, wuk_ref[...], preferred_element_type=jnp.float32).astype(jnp.bfloat16)
    for h in range(N_HEADS):
        k_ref[0, h, :, 0:QK_NOPE_DIM] = k_nope[:, h * QK_NOPE_DIM:(h + 1) * QK_NOPE_DIM]
        k_ref[0, h, :, QK_NOPE_DIM:QK_DIM] = k_rope
    vt_ref[0] = jax.lax.dot_general(wuvt_ref[...], ckv, (((1,), (1,)), ((), ())),
                                    preferred_element_type=jnp.float32).astype(jnp.bfloat16)

    scale = math.log2(math.e) / math.sqrt(QK_DIM)
    qr = jnp.dot(cq, wqr_ref[...], preferred_element_type=jnp.float32)
    n_rep = N_HEADS * QK_ROPE_DIM // LANES
    cos_h = jnp.concatenate([cos] * n_rep, axis=1)
    ssin_h = jnp.concatenate([ssin] * n_rep, axis=1)
    half = QK_ROPE_DIM // 2
    qs = jnp.where(jnp.concatenate([first_half] * n_rep, axis=1),
                   pltpu.roll(qr, qr.shape[1] - half, axis=1), pltpu.roll(qr, half, axis=1))
    q_rope = ((qr * cos_h + qs * ssin_h) * scale).astype(jnp.bfloat16)
    q_nope = (jnp.dot(cq, wq_ref[...], preferred_element_type=jnp.float32) * scale).astype(jnp.bfloat16)
    for h in range(N_HEADS):
        q_ref[0, h, :, 0:QK_NOPE_DIM] = q_nope[:, h * QK_NOPE_DIM:(h + 1) * QK_NOPE_DIM]
        q_ref[0, h, :, QK_NOPE_DIM:QK_DIM] = q_rope[:, h * QK_ROPE_DIM:(h + 1) * QK_ROPE_DIM]


def _mixer_in(x, cos, sin, pre_g, w_in, qg, wq, wqr, kvg, wuk, wuv):
    b, s, d = x.shape
    tm = MIX_IN_TM
    heads = lambda w: pl.BlockSpec((1, N_HEADS, tm, w), lambda bi, i: (bi, 0, i, 0))
    rows = lambda w: pl.BlockSpec((1, tm, w), lambda bi, i: (bi, i, 0))
    return pl.pallas_call(
        _mixer_in_kernel,
        grid=(b, s // tm),
        in_specs=[rows(d), rows(LANES), rows(LANES), _resident((1, d)),
                  pl.BlockSpec((None,) + w_in.shape[1:], lambda *_: (0, 0, 0), pipeline_mode=pl.Buffered(1)),
                  _resident(qg.shape),
                  _resident(wq.shape), _resident(wqr.shape), _resident(kvg.shape),
                  _resident(wuk.shape), _resident(wuv.shape)],
        out_specs=[heads(QK_DIM), heads(QK_DIM), pl.BlockSpec((1, N_HEADS * V_HEAD_DIM, tm), lambda bi, i: (bi, 0, i)), rows(POOL_WIDTH), rows(2 * D_MODEL)],
        out_shape=[jax.ShapeDtypeStruct((b, N_HEADS, s, QK_DIM), jnp.bfloat16),
                   jax.ShapeDtypeStruct((b, N_HEADS, s, QK_DIM), jnp.bfloat16),
                   jax.ShapeDtypeStruct((b, N_HEADS * V_HEAD_DIM, s), jnp.bfloat16),
                   jax.ShapeDtypeStruct((b, s, POOL_WIDTH), jnp.float32),
                   jax.ShapeDtypeStruct((b, s, 2 * D_MODEL), jnp.bfloat16)],
        scratch_shapes=[pltpu.VMEM((d, IN_WIDTH_EXT), jnp.bfloat16)],
        compiler_params=pltpu.CompilerParams(dimension_semantics=("arbitrary", "arbitrary"),
                                             vmem_limit_bytes=VMEM_LIMIT),
        name="mixer_in",
    )(x, cos, sin, pre_g, w_in, qg, wq, wqr, kvg, wuk, wuv)


def _attn_kernel(q_ref, k_ref, vt_ref, o_ref, st_ref, p_ref):
    n_heads, s_len = k_ref.shape[1], k_ref.shape[2]
    n_sub = s_len // ATT_TQ
    n_chunk = s_len // ATT_KC
    tiles = [(hh, j) for hh in range(n_heads) for j in range(n_sub)]

    def logits(t):
        hh, j = tiles[t]
        st = jax.lax.dot_general(k_ref[0, hh], q_ref[0, hh, j * ATT_TQ:(j + 1) * ATT_TQ, :],
                                 (((1,), (1,)), ((), ())), preferred_element_type=jnp.float32)
        st_ref[t % 2] = st
        return jnp.max(st, axis=0, keepdims=True)

    m_next = logits(0)
    for t, (hh, j) in enumerate(tiles):
        slot = t % 2
        m = m_next
        if t + 1 < len(tiles):
            m_next = logits(t + 1)
        l = jnp.zeros_like(m)
        for c in range(n_chunk):
            keys = slice(c * ATT_KC, (c + 1) * ATT_KC)
            p = jnp.exp2(st_ref[slot, keys, :] - m)
            l = l + jnp.sum(p, axis=0, keepdims=True)
            p_ref[slot, keys, :] = p.astype(jnp.bfloat16)
        ot = jnp.dot(vt_ref[0, hh], p_ref[slot], preferred_element_type=jnp.float32)
        o_ref[0, j * ATT_TQ:(j + 1) * ATT_TQ, hh * V_HEAD_DIM:(hh + 1) * V_HEAD_DIM] = (
            (ot / l).T.astype(jnp.bfloat16))


def _attention(q, k, vt):
    b, h, s, _ = q.shape
    hs = ATT_HEADS
    head = lambda r, c: pl.BlockSpec((1, hs, r, c), lambda bi, hi: (bi, hi, 0, 0))
    return pl.pallas_call(
        _attn_kernel,
        grid=(b, h // hs),
        in_specs=[head(s, QK_DIM), head(s, QK_DIM), head(V_HEAD_DIM, s)],
        out_specs=pl.BlockSpec((1, s, hs * V_HEAD_DIM), lambda bi, hi: (bi, 0, hi)),
        out_shape=jax.ShapeDtypeStruct((b, s, h * V_HEAD_DIM), jnp.bfloat16),
        scratch_shapes=[pltpu.VMEM((2, s, ATT_TQ), jnp.float32), pltpu.VMEM((2, s, ATT_TQ), jnp.bfloat16)],
        compiler_params=pltpu.CompilerParams(dimension_semantics=("arbitrary", "arbitrary"),
                                             vmem_limit_bytes=VMEM_LIMIT),
        name="attention",
    )(q, k, vt)


def _mixer_out_kernel(o_ref, pool_ref, prev_ref, next_ref, gate_ref, x_ref, wo_ref, pw_ref, ps_ref, wop_ref,
                      wout_ref, post_g_ref, *rest, seq_len):
    n_w = (len(rest) - 2) // 2
    out_ref, ext_ref = rest[n_w], rest[-1]
    _cast_stream_step(rest[:n_w], rest[n_w + 1:-1])
    i = pl.program_id(1)
    n_i = pl.num_programs(1)
    tm = pool_ref.shape[1]

    ext_ref[0:POOL_HALO, :] = jnp.where(i > 0, prev_ref[0], 0.0)
    ext_ref[POOL_HALO:POOL_HALO + tm, :] = pool_ref[0]
    ext_ref[POOL_HALO + tm:POOL_HALO + tm + POOL_HALO, :] = jnp.where(i < n_i - 1, next_ref[0], 0.0)

    win = MIX_SUB + 2 * POOL_HALO
    for r in range(tm // MIX_SUB):
        r0 = r * MIX_SUB
        rows = slice(r0, r0 + MIX_SUB)
        o_rows = o_ref[0, rows, :]
        t_first = i * tm + r0 + jax.lax.broadcasted_iota(jnp.int32, (POOL_HALO, 1), 0)
        t_last = t_first + (MIX_SUB - POOL_HALO)
        ys, y_attn_parts = [], []
        n_col = D_MODEL // len(POOL_WINDOWS)
        for gi, w in enumerate(POOL_WINDOWS):
            y_attn_parts.append(jnp.dot(o_rows, wo_ref[:, gi * n_col:(gi + 1) * n_col],
                                        preferred_element_type=jnp.float32))
            left = w // 2
            right = w - 1 - left
            cols = slice(gi * POOL_GROUP, (gi + 1) * POOL_GROUP)
            xe = ext_ref[r0:r0 + win, cols]
            fwd, span = xe, 1
            while span < min(w, POOL_HALO):
                fwd = fwd + pltpu.roll(fwd, win - span, axis=0)
                span *= 2
            if w > span:
                fwd = fwd + pltpu.roll(fwd, win - span, axis=0)
            total = pltpu.roll(fwd, left, axis=0)[POOL_HALO:POOL_HALO + MIX_SUB]
            inv = [1.0 / (jnp.minimum(tt + right + 1, seq_len) - jnp.maximum(tt - left, 0)).astype(jnp.float32)
                   for tt in (t_first, t_last)]
            mean = jnp.concatenate([total[:POOL_HALO] * inv[0], total[POOL_HALO:-POOL_HALO] * (1.0 / w),
                                    total[-POOL_HALO:] * inv[1]], axis=0)
            dg = mean - xe[POOL_HALO:POOL_HALO + MIX_SUB]
            ys.append(_bdot(dg, pw_ref[gi]))
        y = jnp.concatenate(ys, axis=1) * ps_ref[...]
        y_pool = _bdot(y, wop_ref[...])
        y_attn = jnp.concatenate(y_attn_parts, axis=1)

        g_attn = gate_ref[0, rows, 0:D_MODEL].astype(jnp.float32)
        g_pool = gate_ref[0, rows, D_MODEL:2 * D_MODEL].astype(jnp.float32)
        mixed = _bdot(g_attn * y_attn + g_pool * y_pool, wout_ref[...])
        out_ref[0, rows, :] = x_ref[0, rows, :] + _rmsnorm(mixed, post_g_ref[...])


def _mixer_out(o, pool, gate, x, wo, pw, ps, wop, wout, post_g, weights):
    b, s, d = x.shape
    tm = MIX_OUT_TM
    nb = tm // POOL_HALO
    last = s // POOL_HALO - 1
    n_i = s // tm
    w_in_specs, w_out_specs, w_shapes = _cast_stream_specs(weights, b * n_i, lambda bi, i: bi * n_i + i)
    rows = lambda w: pl.BlockSpec((1, tm, w), lambda bi, i: (bi, i, 0))
    prev = pl.BlockSpec((1, POOL_HALO, POOL_WIDTH), lambda bi, i: (bi, jnp.maximum(i * nb - 1, 0), 0))
    nxt = pl.BlockSpec((1, POOL_HALO, POOL_WIDTH), lambda bi, i: (bi, jnp.minimum((i + 1) * nb, last), 0))
    return pl.pallas_call(
        functools.partial(_mixer_out_kernel, seq_len=s),
        grid=(b, s // tm),
        in_specs=[rows(d), rows(POOL_WIDTH), prev, nxt, rows(2 * d), rows(d), _resident(wo.shape),
                  _resident(pw.shape), _resident(ps.shape), _resident(wop.shape), _resident(wout.shape),
                  _resident(post_g.shape)] + w_in_specs,
        out_specs=[rows(d)] + w_out_specs,
        out_shape=[jax.ShapeDtypeStruct((b, s, d), jnp.float32)] + w_shapes,
        scratch_shapes=[pltpu.VMEM((tm + 2 * POOL_HALO, POOL_WIDTH), jnp.float32)],
        compiler_params=pltpu.CompilerParams(dimension_semantics=("arbitrary", "arbitrary"),
                                             vmem_limit_bytes=VMEM_LIMIT),
        name="mixer_out",
    )(o, pool, pool, pool, gate, x, wo, pw, ps, wop, wout, post_g, *weights)


def _split_w_uq(w_uq):
    r = w_uq.shape[0]
    w3 = w_uq.reshape(r, N_HEADS, QK_DIM)
    nope = w3[:, :, :QK_NOPE_DIM].reshape(r, N_HEADS * QK_NOPE_DIM)
    return nope, w3[:, :, QK_NOPE_DIM:].reshape(r, N_HEADS * QK_ROPE_DIM)


def kernel(x, positions, ffn1_pre_g, ffn1_w_gate, ffn1_w_up, ffn1_w_down, ffn1_post_g, mix_pre_g, w_in, q_a_norm_g, w_uq, kv_a_norm_g, w_uk, w_uv, w_o_attn, pool_w, pool_scale, w_o_pool, w_out, mix_post_g, ffn2_pre_g, ffn2_w_gate, ffn2_w_up, ffn2_w_down, ffn2_post_g, final_g):
    b, s, d = x.shape
    bf = lambda w: w.astype(jnp.bfloat16)
    row = lambda g: g.reshape(1, -1)
    inv_freq = ROPE_THETA ** (-jnp.arange(0, QK_ROPE_DIM, 2, dtype=jnp.float32) / QK_ROPE_DIM)
    n_freq = QK_ROPE_DIM // 2
    invf = jnp.tile(inv_freq, LANES // n_freq).reshape(1, LANES)
    pos_rep = jnp.repeat(positions.reshape(b * s * n_freq // LANES, LANES // n_freq), n_freq, axis=1)
    assert w_in.shape[0] == 1, "one layer: the second FFN's weight casts are hosted by the mixer_out before it"
    l = 0

    small = [w_o_attn[l], w_out[l], w_o_pool[l], w_uk[l], pool_w[l].reshape(POOL_WIDTH, POOL_GROUP)]
    x1, cos, sin, wo_b, wout_b, wop_b, wuk_b, pw_b = _ffn(
        x.reshape(b * s, d), row(ffn1_pre_g[l]), ffn1_w_gate[l], ffn1_w_up[l], ffn1_w_down[l],
        row(ffn1_post_g[l]), row(final_g[l]), False, rope=(pos_rep, invf), cast_weights=small)
    x1, cos, sin = x1.reshape(b, s, d), cos.reshape(b, s, LANES), sin.reshape(b, s, LANES)
    wq, wqr = _split_w_uq(bf(w_uq[l]))
    q, k, vt, pool, gate = _mixer_in(
        x1, cos, sin, row(mix_pre_g[l]), w_in, row(q_a_norm_g[l]),
        wq, wqr, row(kv_a_norm_g[l]), wuk_b, bf(w_uv[l]).T)
    o = _attention(q, k, vt.reshape(b, N_HEADS, V_HEAD_DIM, s))
    x2, *ffn2_w = _mixer_out(o, pool, gate, x1, wo_b, pw_b.reshape(pool_w.shape[1:]), row(pool_scale[l]),
                             wop_b, wout_b, row(mix_post_g[l]),
                             [ffn2_w_gate[l], ffn2_w_up[l], ffn2_w_down[l]])
    return _ffn(x2.reshape(b * s, d), row(ffn2_pre_g[l]), *ffn2_w, row(ffn2_post_g[l]), row(final_g[l]),
                True).reshape(b, s, d)
```

```python
import functools
import math

import jax
import jax.numpy as jnp
from jax.experimental import pallas as pl
from jax.experimental.pallas import tpu as pltpu

D_MODEL = 1024
N_HEADS = 8
QK_NOPE_DIM = 128
QK_ROPE_DIM = 64
QK_DIM = QK_NOPE_DIM + QK_ROPE_DIM
V_HEAD_DIM = 128
Q_LORA_RANK = 384
KV_LORA_RANK = 256
ROPE_THETA = 10000.0
POOL_WINDOWS = (2, 4, 8, 16)
POOL_GROUP = 128
POOL_WIDTH = POOL_GROUP * len(POOL_WINDOWS)
D_FF = 2816
MACARON_WEIGHT = 0.5
NORM_EPS = 1e-6

LANES = 128
SUBLANES = 8
POOL_HALO = 8

OFF_CQ = 0
OFF_CKV = OFF_CQ + Q_LORA_RANK
OFF_KR = OFF_CKV + KV_LORA_RANK
OFF_POOL = OFF_KR + 2 * QK_ROPE_DIM
OFF_GATE = OFF_POOL + POOL_WIDTH
IN_WIDTH_EXT = OFF_GATE + 2 * D_MODEL
W_IN_ROWS = 64
FFN_W_STEPS = 11
ROPE_TABLE_ROWS = 512

FFN_TM = 1024
FFN_SUB = 256
MIX_IN_TM = 512
MIX_OUT_TM = 1024
MIX_SUB = 256
ATT_TQ = 512
ATT_KC = 256
ATT_HEADS = 4
VMEM_LIMIT = 56 * 1024 * 1024


def _resident(shape):
    return pl.BlockSpec(shape, lambda *_: (0,) * len(shape), pipeline_mode=pl.Buffered(1))


def _rmsnorm(x, g):
    ms = jnp.mean(x * x, axis=-1, keepdims=True)
    return x * jax.lax.rsqrt(ms + NORM_EPS) * g


def _bdot(a, b):
    return jnp.dot(a.astype(jnp.bfloat16), b, preferred_element_type=jnp.float32)


def _rope_table_block(pos_ref, invf_ref, cos_ref, sin_ref):
    ang = pos_ref[...].astype(jnp.float32) * invf_ref[...]
    n_freq = QK_ROPE_DIM // 2
    tok_per_row = LANES // n_freq
    rows = ang.shape[0]
    for table, out_ref in ((jnp.cos(ang), cos_ref), (jnp.sin(ang), sin_ref)):
        for k in range(tok_per_row):
            piece = table[:, k * n_freq:(k + 1) * n_freq]
            out_ref[pl.ds(k, rows, stride=tok_per_row), :] = jnp.concatenate([piece] * tok_per_row, axis=1)


def _ffn_kernel(x_ref, pre_g_ref, wg_ref, wu_ref, wd_ref, post_g_ref, final_g_ref, *rest, final_norm, w_steps,
                rope_steps):
    if rope_steps:
        pos_ref, invf_ref, o_ref, cos_ref, sin_ref, *w_scratch = rest
    else:
        o_ref, *w_scratch = rest
    weights = w_scratch if w_steps else (wg_ref, wu_ref, wd_ref)

    def row_tile():
        for r in range(x_ref.shape[0] // FFN_SUB):
            rows = slice(r * FFN_SUB, (r + 1) * FFN_SUB)
            x = x_ref[rows, :]
            xn = _rmsnorm(x, pre_g_ref[...]).astype(jnp.bfloat16)
            g = jnp.dot(xn, weights[0][...], preferred_element_type=jnp.float32)
            u = jnp.dot(xn, weights[1][...], preferred_element_type=jnp.float32)
            h = g * (1.0 / (1.0 + jnp.exp(-g))) * u
            f = _bdot(h, weights[2][...])
            y = x + MACARON_WEIGHT * _rmsnorm(f, post_g_ref[...])
            if final_norm:
                y = _rmsnorm(y, final_g_ref[...])
            o_ref[rows, :] = y

    if not w_steps:
        row_tile()
        return
    i = pl.program_id(0)
    fc = wg_ref.shape[1]
    for c in range(w_steps):
        @pl.when(i == c)
        def _(c=c):
            cols = slice(c * fc, (c + 1) * fc)
            weights[0][:, cols] = wg_ref[...].astype(jnp.bfloat16)
            weights[1][:, cols] = wu_ref[...].astype(jnp.bfloat16)
            weights[2][cols, :] = wd_ref[...].astype(jnp.bfloat16)
    if rope_steps:
        pl.when(i < rope_steps)(lambda: _rope_table_block(pos_ref, invf_ref, cos_ref, sin_ref))
    pl.when(i >= w_steps)(row_tile)


def _ffn(x2d, pre_g, wg, wu, wd, post_g, final_g, final_norm, rope=None):
    n, d = x2d.shape
    f = wg.shape[1]
    tm = FFN_TM
    w_steps = 0 if wg.dtype == jnp.bfloat16 else FFN_W_STEPS
    rope_steps, rope_in, rope_specs, rope_out_specs, rope_shapes = 0, [], [], [], []
    if rope is not None:
        pos_rep, invf = rope
        tr = ROPE_TABLE_ROWS
        tok_per_row = LANES // (QK_ROPE_DIM // 2)
        rope_steps = pos_rep.shape[0] // tr
        assert 0 < rope_steps <= w_steps
        blk = lambda i: (jnp.minimum(i, rope_steps - 1), 0)
        rope_in = [pos_rep, invf]
        rope_specs = [pl.BlockSpec((tr, LANES), blk), _resident(invf.shape)]
        rope_out_specs = [pl.BlockSpec((tr * tok_per_row, LANES), blk)] * 2
        rope_shapes = [jax.ShapeDtypeStruct((pos_rep.shape[0] * tok_per_row, LANES), jnp.float32)] * 2
    if w_steps:
        fc = f // w_steps
        chunk = lambda i: jnp.minimum(i, w_steps - 1)
        w_specs = [pl.BlockSpec((d, fc), lambda i: (0, chunk(i))), pl.BlockSpec((d, fc), lambda i: (0, chunk(i))),
                   pl.BlockSpec((fc, d), lambda i: (chunk(i), 0))]
        w_scratch = [pltpu.VMEM((d, f), jnp.bfloat16), pltpu.VMEM((d, f), jnp.bfloat16),
                     pltpu.VMEM((f, d), jnp.bfloat16)]
    else:
        w_specs, w_scratch = [_resident(wg.shape), _resident(wu.shape), _resident(wd.shape)], []
    row = pl.BlockSpec((tm, d), lambda i: (jnp.maximum(i - w_steps, 0), 0))
    outs = pl.pallas_call(
        functools.partial(_ffn_kernel, final_norm=final_norm, w_steps=w_steps, rope_steps=rope_steps),
        grid=(w_steps + n // tm,),
        in_specs=[row, _resident((1, d))] + w_specs + [_resident((1, d)), _resident((1, d))] + rope_specs,
        out_specs=[row] + rope_out_specs,
        out_shape=[jax.ShapeDtypeStruct((n, d), jnp.float32)] + rope_shapes,
        scratch_shapes=w_scratch,
        compiler_params=pltpu.CompilerParams(dimension_semantics=("arbitrary",), vmem_limit_bytes=VMEM_LIMIT),
        name="ffn_final" if final_norm else "ffn",
    )(x2d, pre_g, wg, wu, wd, post_g, final_g, *rope_in)
    return outs if rope_steps else outs[0]


def _cast_stream_specs(weights, n_steps, step_of):
    spec = lambda w: pl.BlockSpec((w.shape[0] // n_steps, w.shape[1]), lambda *g: (step_of(*g), 0))
    specs = [spec(w) for w in weights]
    return specs, specs, [jax.ShapeDtypeStruct(w.shape, jnp.bfloat16) for w in weights]


def _cast_stream_step(in_refs, out_refs):
    for src, dst in zip(in_refs, out_refs):
        dst[...] = src[...].astype(jnp.bfloat16)


def _extend_w_in(w_ref, wext_ref):
    kr1 = OFF_KR + QK_ROPE_DIM
    half = QK_ROPE_DIM // 2
    n_in = w_ref.shape[1]

    def body(rb, carry):
        rows = pl.ds(pl.multiple_of(rb * W_IN_ROWS, W_IN_ROWS), W_IN_ROWS)
        wext_ref[rows, 0:kr1] = w_ref[rows, 0:kr1].astype(jnp.bfloat16)
        wext_ref[rows, kr1:kr1 + half] = w_ref[rows, OFF_KR + half:kr1].astype(jnp.bfloat16)
        wext_ref[rows, kr1 + half:OFF_POOL] = w_ref[rows, OFF_KR:OFF_KR + half].astype(jnp.bfloat16)
        wext_ref[rows, OFF_POOL:IN_WIDTH_EXT] = w_ref[rows, kr1:n_in].astype(jnp.bfloat16)
        return carry

    jax.lax.fori_loop(0, w_ref.shape[0] // W_IN_ROWS, body, 0)


def _mixer_in_kernel(x_ref, cos_ref, sin_ref, pre_g_ref, w_in_ref, *refs):
    weights, (q_ref, k_ref, vt_ref, pool_ref, gate_ref), wext_ref = refs[:-6], refs[-6:-1], refs[-1]

    @pl.when((pl.program_id(0) == 0) & (pl.program_id(1) == 0))
    def _():
        _extend_w_in(w_in_ref, wext_ref)

    weights = (pre_g_ref, wext_ref) + tuple(weights)
    for r in range(x_ref.shape[1] // MIX_SUB):
        rows = pl.ds(r * MIX_SUB, MIX_SUB)
        _mixer_in_tile(x_ref.at[:, rows, :], cos_ref.at[:, rows, :], sin_ref.at[:, rows, :], *weights,
                       q_ref.at[:, :, rows, :], k_ref.at[:, :, rows, :], vt_ref.at[:, :, rows],
                       pool_ref.at[:, rows, :], gate_ref.at[:, rows, :])


def _mixer_in_tile(x_ref, cos_ref, sin_ref, pre_g_ref, w_in_ref, qg_ref, wq_ref, wqr_ref, kvg_ref, wuk_ref, wuvt_ref,
                   q_ref, k_ref, vt_ref, pool_ref, gate_ref):
    u = _rmsnorm(x_ref[0], pre_g_ref[...]).astype(jnp.bfloat16)
    z = jnp.dot(u, w_in_ref[:, :OFF_GATE], preferred_element_type=jnp.float32)
    pool_ref[0] = z[:, OFF_POOL:OFF_POOL + POOL_WIDTH]

    gl = jnp.dot(u, w_in_ref[:, OFF_GATE:], preferred_element_type=jnp.float32)
    gate_ref[0] = (1.0 / (1.0 + jnp.exp(-gl))).astype(jnp.bfloat16)

    cos = cos_ref[0]
    sin = sin_ref[0]
    lane = jax.lax.broadcasted_iota(jnp.int32, cos.shape, 1)
    first_half = (lane % QK_ROPE_DIM) < (QK_ROPE_DIM // 2)
    ssin = jnp.where(first_half, -sin, sin)

    kr2 = z[:, OFF_KR:OFF_KR + LANES] * jnp.where(lane < QK_ROPE_DIM, cos, ssin)
    k_rope = (kr2[:, :QK_ROPE_DIM] + kr2[:, QK_ROPE_DIM:]).astype(jnp.bfloat16)
    ckv = _rmsnorm(z[:, OFF_CKV:OFF_CKV + KV_LORA_RANK], kvg_ref[...]).astype(jnp.bfloat16)
    cq = _rmsnorm(z[:, OFF_CQ:OFF_CQ + Q_LORA_RANK], qg_ref[...]).astype(jnp.bfloat16)

    k_nope = jnp.dot(ckv, wuk_ref[...], preferred_element_type=jnp.float32).astype(jnp.bfloat16)
    for h in range(N_HEADS):
        k_ref[0, h, :, 0:QK_NOPE_DIM] = k_nope[:, h * QK_NOPE_DIM:(h + 1) * QK_NOPE_DIM]
        k_ref[0, h, :, QK_NOPE_DIM:QK_DIM] = k_rope
    vt_ref[0] = jax.lax.dot_general(wuvt_ref[...], ckv, (((1,), (1,)), ((), ())),
                                    preferred_element_type=jnp.float32).astype(jnp.bfloat16)

    scale = math.log2(math.e) / math.sqrt(QK_DIM)
    qr = jnp.dot(cq, wqr_ref[...], preferred_element_type=jnp.float32)
    n_rep = N_HEADS * QK_ROPE_DIM // LANES
    cos_h = jnp.concatenate([cos] * n_rep, axis=1)
    ssin_h = jnp.concatenate([ssin] * n_rep, axis=1)
    half = QK_ROPE_DIM // 2
    qs = jnp.where(jnp.concatenate([first_half] * n_rep, axis=1),
                   pltpu.roll(qr, qr.shape[1] - half, axis=1), pltpu.roll(qr, half, axis=1))
    q_rope = ((qr * cos_h + qs * ssin_h) * scale).astype(jnp.bfloat16)
    q_nope = (jnp.dot(cq, wq_ref[...], preferred_element_type=jnp.float32) * scale).astype(jnp.bfloat16)
    for h in range(N_HEADS):
        q_ref[0, h, :, 0:QK_NOPE_DIM] = q_nope[:, h * QK_NOPE_DIM:(h + 1) * QK_NOPE_DIM]
        q_ref[0, h, :, QK_NOPE_DIM:QK_DIM] = q_rope[:, h * QK_ROPE_DIM:(h + 1) * QK_ROPE_DIM]


def _mixer_in(x, cos, sin, pre_g, w_in, qg, wq, wqr, kvg, wuk, wuv):
    b, s, d = x.shape
    tm = MIX_IN_TM
    heads = lambda w: pl.BlockSpec((1, N_HEADS, tm, w), lambda bi, i: (bi, 0, i, 0))
    rows = lambda w: pl.BlockSpec((1, tm, w), lambda bi, i: (bi, i, 0))
    return pl.pallas_call(
        _mixer_in_kernel,
        grid=(b, s // tm),
        in_specs=[rows(d), rows(LANES), rows(LANES), _resident((1, d)),
                  pl.BlockSpec((None,) + w_in.shape[1:], lambda *_: (0, 0, 0), pipeline_mode=pl.Buffered(1)),
                  _resident(qg.shape),
                  _resident(wq.shape), _resident(wqr.shape), _resident(kvg.shape),
                  _resident(wuk.shape), _resident(wuv.shape)],
        out_specs=[heads(QK_DIM), heads(QK_DIM), pl.BlockSpec((1, N_HEADS * V_HEAD_DIM, tm), lambda bi, i: (bi, 0, i)), rows(POOL_WIDTH), rows(2 * D_MODEL)],
        out_shape=[jax.ShapeDtypeStruct((b, N_HEADS, s, QK_DIM), jnp.bfloat16),
                   jax.ShapeDtypeStruct((b, N_HEADS, s, QK_DIM), jnp.bfloat16),
                   jax.ShapeDtypeStruct((b, N_HEADS * V_HEAD_DIM, s), jnp.bfloat16),
                   jax.ShapeDtypeStruct((b, s, POOL_WIDTH), jnp.float32),
                   jax.ShapeDtypeStruct((b, s, 2 * D_MODEL), jnp.bfloat16)],
        scratch_shapes=[pltpu.VMEM((d, IN_WIDTH_EXT), jnp.bfloat16)],
        compiler_params=pltpu.CompilerParams(dimension_semantics=("arbitrary", "arbitrary"),
                                             vmem_limit_bytes=VMEM_LIMIT),
        name="mixer_in",
    )(x, cos, sin, pre_g, w_in, qg, wq, wqr, kvg, wuk, wuv)


def _attn_kernel(q_ref, k_ref, vt_ref, o_ref, st_ref, p_ref):
    n_heads, s_len = k_ref.shape[1], k_ref.shape[2]
    n_sub = s_len // ATT_TQ
    n_chunk = s_len // ATT_KC
    tiles = [(hh, j) for hh in range(n_heads) for j in range(n_sub)]

    def logits(t):
        hh, j = tiles[t]
        st = jax.lax.dot_general(k_ref[0, hh], q_ref[0, hh, j * ATT_TQ:(j + 1) * ATT_TQ, :],
                                 (((1,), (1,)), ((), ())), preferred_element_type=jnp.float32)
        st_ref[t % 2] = st
        return jnp.max(st, axis=0, keepdims=True)

    m_next = logits(0)
    for t, (hh, j) in enumerate(tiles):
        slot = t % 2
        m = m_next
        if t + 1 < len(tiles):
            m_next = logits(t + 1)
        l8 = jnp.zeros((SUBLANES, ATT_TQ), jnp.float32)
        for c in range(n_chunk):
            keys = slice(c * ATT_KC, (c + 1) * ATT_KC)
            p = jnp.exp2(st_ref[slot, keys, :] - m)
            l8 = l8 + jnp.sum(p.reshape(ATT_KC // SUBLANES, SUBLANES, ATT_TQ), axis=0)
            p_ref[slot, keys, :] = p.astype(jnp.bfloat16)
        l = jnp.sum(l8, axis=0, keepdims=True)
        ot = jnp.dot(vt_ref[0, hh], p_ref[slot], preferred_element_type=jnp.float32)
        o_ref[0, j * ATT_TQ:(j + 1) * ATT_TQ, hh * V_HEAD_DIM:(hh + 1) * V_HEAD_DIM] = (
            (ot / l).T.astype(jnp.bfloat16))


def _attention(q, k, vt):
    b, h, s, _ = q.shape
    hs = ATT_HEADS
    head = lambda r, c: pl.BlockSpec((1, hs, r, c), lambda bi, hi: (bi, hi, 0, 0))
    return pl.pallas_call(
        _attn_kernel,
        grid=(b, h // hs),
        in_specs=[head(s, QK_DIM), head(s, QK_DIM), head(V_HEAD_DIM, s)],
        out_specs=pl.BlockSpec((1, s, hs * V_HEAD_DIM), lambda bi, hi: (bi, 0, hi)),
        out_shape=jax.ShapeDtypeStruct((b, s, h * V_HEAD_DIM), jnp.bfloat16),
        scratch_shapes=[pltpu.VMEM((2, s, ATT_TQ), jnp.float32), pltpu.VMEM((2, s, ATT_TQ), jnp.bfloat16)],
        compiler_params=pltpu.CompilerParams(dimension_semantics=("arbitrary", "arbitrary"),
                                             vmem_limit_bytes=VMEM_LIMIT),
        name="attention",
    )(q, k, vt)


def _mixer_out_kernel(o_ref, pool_ref, prev_ref, next_ref, gate_ref, x_ref, wo_ref, pw_ref, ps_ref, wop_ref,
                      wout_ref, post_g_ref, *rest, seq_len):
    n_w = (len(rest) - 2) // 2
    out_ref, ext_ref = rest[n_w], rest[-1]
    _cast_stream_step(rest[:n_w], rest[n_w + 1:-1])
    i = pl.program_id(1)
    n_i = pl.num_programs(1)
    tm = pool_ref.shape[1]

    ext_ref[0:POOL_HALO, :] = jnp.where(i > 0, prev_ref[0], 0.0)
    ext_ref[POOL_HALO:POOL_HALO + tm, :] = pool_ref[0]
    ext_ref[POOL_HALO + tm:POOL_HALO + tm + POOL_HALO, :] = jnp.where(i < n_i - 1, next_ref[0], 0.0)

    win = MIX_SUB + 2 * POOL_HALO
    for r in range(tm // MIX_SUB):
        r0 = r * MIX_SUB
        rows = slice(r0, r0 + MIX_SUB)
        o_rows = o_ref[0, rows, :]
        t_first = i * tm + r0 + jax.lax.broadcasted_iota(jnp.int32, (POOL_HALO, 1), 0)
        t_last = t_first + (MIX_SUB - POOL_HALO)
        ys, y_attn_parts = [], []
        n_col = D_MODEL // len(POOL_WINDOWS)
        for gi, w in enumerate(POOL_WINDOWS):
            y_attn_parts.append(jnp.dot(o_rows, wo_ref[:, gi * n_col:(gi + 1) * n_col],
                                        preferred_element_type=jnp.float32))
            left = w // 2
            right = w - 1 - left
            cols = slice(gi * POOL_GROUP, (gi + 1) * POOL_GROUP)
            xe = ext_ref[r0:r0 + win, cols]
            fwd, span = xe, 1
            while span < min(w, POOL_HALO):
                fwd = fwd + pltpu.roll(fwd, win - span, axis=0)
                span *= 2
            if w > span:
                fwd = fwd + pltpu.roll(fwd, win - span, axis=0)
            total = pltpu.roll(fwd, left, axis=0)[POOL_HALO:POOL_HALO + MIX_SUB]
            inv = [1.0 / (jnp.minimum(tt + right + 1, seq_len) - jnp.maximum(tt - left, 0)).astype(jnp.float32)
                   for tt in (t_first, t_last)]
            mean = jnp.concatenate([total[:POOL_HALO] * inv[0], total[POOL_HALO:-POOL_HALO] * (1.0 / w),
                                    total[-POOL_HALO:] * inv[1]], axis=0)
            dg = mean - xe[POOL_HALO:POOL_HALO + MIX_SUB]
            ys.append(_bdot(dg, pw_ref[gi]))
        y = jnp.concatenate(ys, axis=1) * ps_ref[...]
        y_pool = _bdot(y, wop_ref[...])
        y_attn = jnp.concatenate(y_attn_parts, axis=1)

        g_attn = gate_ref[0, rows, 0:D_MODEL].astype(jnp.float32)
        g_pool = gate_ref[0, rows, D_MODEL:2 * D_MODEL].astype(jnp.float32)
        mixed = _bdot(g_attn * y_attn + g_pool * y_pool, wout_ref[...])
        out_ref[0, rows, :] = x_ref[0, rows, :] + _rmsnorm(mixed, post_g_ref[...])


def _mixer_out(o, pool, gate, x, wo, pw, ps, wop, wout, post_g, weights):
    b, s, d = x.shape
    tm = MIX_OUT_TM
    nb = tm // POOL_HALO
    last = s // POOL_HALO - 1
    n_i = s // tm
    w_in_specs, w_out_specs, w_shapes = _cast_stream_specs(weights, b * n_i, lambda bi, i: bi * n_i + i)
    rows = lambda w: pl.BlockSpec((1, tm, w), lambda bi, i: (bi, i, 0))
    prev = pl.BlockSpec((1, POOL_HALO, POOL_WIDTH), lambda bi, i: (bi, jnp.maximum(i * nb - 1, 0), 0))
    nxt = pl.BlockSpec((1, POOL_HALO, POOL_WIDTH), lambda bi, i: (bi, jnp.minimum((i + 1) * nb, last), 0))
    return pl.pallas_call(
        functools.partial(_mixer_out_kernel, seq_len=s),
        grid=(b, s // tm),
        in_specs=[rows(d), rows(POOL_WIDTH), prev, nxt, rows(2 * d), rows(d), _resident(wo.shape),
                  _resident(pw.shape), _resident(ps.shape), _resident(wop.shape), _resident(wout.shape),
                  _resident(post_g.shape)] + w_in_specs,
        out_specs=[rows(d)] + w_out_specs,
        out_shape=[jax.ShapeDtypeStruct((b, s, d), jnp.float32)] + w_shapes,
        scratch_shapes=[pltpu.VMEM((tm + 2 * POOL_HALO, POOL_WIDTH), jnp.float32)],
        compiler_params=pltpu.CompilerParams(dimension_semantics=("arbitrary", "arbitrary"),
                                             vmem_limit_bytes=VMEM_LIMIT),
        name="mixer_out",
    )(o, pool, pool, pool, gate, x, wo, pw, ps, wop, wout, post_g, *weights)


def _split_w_uq(w_uq):
    r = w_uq.shape[0]
    w3 = w_uq.reshape(r, N_HEADS, QK_DIM)
    nope = w3[:, :, :QK_NOPE_DIM].reshape(r, N_HEADS * QK_NOPE_DIM)
    return nope, w3[:, :, QK_NOPE_DIM:].reshape(r, N_HEADS * QK_ROPE_DIM)


def kernel(x, positions, ffn1_pre_g, ffn1_w_gate, ffn1_w_up, ffn1_w_down, ffn1_post_g, mix_pre_g, w_in, q_a_norm_g, w_uq, kv_a_norm_g, w_uk, w_uv, w_o_attn, pool_w, pool_scale, w_o_pool, w_out, mix_post_g, ffn2_pre_g, ffn2_w_gate, ffn2_w_up, ffn2_w_down, ffn2_post_g, final_g):
    b, s, d = x.shape
    bf = lambda w: w.astype(jnp.bfloat16)
    row = lambda g: g.reshape(1, -1)
    inv_freq = ROPE_THETA ** (-jnp.arange(0, QK_ROPE_DIM, 2, dtype=jnp.float32) / QK_ROPE_DIM)
    n_freq = QK_ROPE_DIM // 2
    invf = jnp.tile(inv_freq, LANES // n_freq).reshape(1, LANES)
    pos_rep = jnp.repeat(positions.reshape(b * s * n_freq // LANES, LANES // n_freq), n_freq, axis=1)
    assert w_in.shape[0] == 1, "one layer: the second FFN's weight casts are hosted by the mixer_out before it"
    l = 0

    x1, cos, sin = _ffn(x.reshape(b * s, d), row(ffn1_pre_g[l]), ffn1_w_gate[l], ffn1_w_up[l], ffn1_w_down[l],
                        row(ffn1_post_g[l]), row(final_g[l]), False, rope=(pos_rep, invf))
    x1, cos, sin = x1.reshape(b, s, d), cos.reshape(b, s, LANES), sin.reshape(b, s, LANES)
    wq, wqr = _split_w_uq(bf(w_uq[l]))
    q, k, vt, pool, gate = _mixer_in(
        x1, cos, sin, row(mix_pre_g[l]), w_in, row(q_a_norm_g[l]),
        wq, wqr, row(kv_a_norm_g[l]), bf(w_uk[l]), bf(w_uv[l]).T)
    o = _attention(q, k, vt.reshape(b, N_HEADS, V_HEAD_DIM, s))
    x2, *ffn2_w = _mixer_out(o, pool, gate, x1, bf(w_o_attn[l]), bf(pool_w[l]), row(pool_scale[l]),
                             bf(w_o_pool[l]), bf(w_out[l]), row(mix_post_g[l]),
                             [ffn2_w_gate[l], ffn2_w_up[l], ffn2_w_down[l]])
    return _ffn(x2.reshape(b * s, d), row(ffn2_pre_g[l]), *ffn2_w, row(ffn2_post_g[l]), row(final_g[l]),
                True).reshape(b, s, d)
```

```python
import functools
import math

import jax
import jax.numpy as jnp
from jax.experimental import pallas as pl
from jax.experimental.pallas import tpu as pltpu

D_MODEL = 1024
N_HEADS = 8
QK_NOPE_DIM = 128
QK_ROPE_DIM = 64
QK_DIM = QK_NOPE_DIM + QK_ROPE_DIM
V_HEAD_DIM = 128
Q_LORA_RANK = 384
KV_LORA_RANK = 256
ROPE_THETA = 10000.0
POOL_WINDOWS = (2, 4, 8, 16)
POOL_GROUP = 128
POOL_WIDTH = POOL_GROUP * len(POOL_WINDOWS)
D_FF = 2816
MACARON_WEIGHT = 0.5
NORM_EPS = 1e-6

LANES = 128
SUBLANES = 8
POOL_HALO = 8

OFF_CQ = 0
OFF_CKV = OFF_CQ + Q_LORA_RANK
OFF_KR = OFF_CKV + KV_LORA_RANK
OFF_POOL = OFF_KR + 2 * QK_ROPE_DIM
OFF_GATE = OFF_POOL + POOL_WIDTH
IN_WIDTH_EXT = OFF_GATE + 2 * D_MODEL
W_IN_ROWS = 64
FFN_W_STEPS = 11
ROPE_TABLE_ROWS = 512

FFN_TM = 1024
FFN_SUB = 256
MIX_IN_TM = 512
MIX_OUT_TM = 1024
MIX_SUB = 256
ATT_TQ = 512
ATT_KC = 256
ATT_HEADS = 4
VMEM_LIMIT = 56 * 1024 * 1024


def _resident(shape):
    return pl.BlockSpec(shape, lambda *_: (0,) * len(shape), pipeline_mode=pl.Buffered(1))


def _rmsnorm(x, g):
    ms = jnp.mean(x * x, axis=-1, keepdims=True)
    return x * jax.lax.rsqrt(ms + NORM_EPS) * g


def _bdot(a, b):
    return jnp.dot(a.astype(jnp.bfloat16), b, preferred_element_type=jnp.float32)


def _rope_table_block(pos_ref, invf_ref, cos_ref, sin_ref):
    n_freq = QK_ROPE_DIM // 2
    tok_per_row = LANES // n_freq
    p4 = pos_ref[...]
    rows = p4.shape[0]
    tok = jax.lax.broadcasted_iota(jnp.int32, (rows, LANES), 1) // n_freq
    pos = p4[:, tok_per_row - 1:tok_per_row]
    for k in range(tok_per_row - 2, -1, -1):
        pos = jnp.where(tok == k, p4[:, k:k + 1], pos)
    ang = pos.astype(jnp.float32) * invf_ref[...]
    for table, out_ref in ((jnp.cos(ang), cos_ref), (jnp.sin(ang), sin_ref)):
        for k in range(tok_per_row):
            piece = table[:, k * n_freq:(k + 1) * n_freq]
            out_ref[pl.ds(k, rows, stride=tok_per_row), :] = jnp.concatenate([piece] * tok_per_row, axis=1)


def _ffn_kernel(x_ref, pre_g_ref, wg_ref, wu_ref, wd_ref, post_g_ref, final_g_ref, *rest, final_norm, w_steps,
                rope_steps):
    if rope_steps:
        pos_ref, invf_ref, o_ref, cos_ref, sin_ref, *w_scratch = rest
    else:
        o_ref, *w_scratch = rest
    weights = w_scratch if w_steps else (wg_ref, wu_ref, wd_ref)

    def row_tile():
        for r in range(x_ref.shape[0] // FFN_SUB):
            rows = slice(r * FFN_SUB, (r + 1) * FFN_SUB)
            x = x_ref[rows, :]
            xn = _rmsnorm(x, pre_g_ref[...]).astype(jnp.bfloat16)
            g = jnp.dot(xn, weights[0][...], preferred_element_type=jnp.float32)
            u = jnp.dot(xn, weights[1][...], preferred_element_type=jnp.float32)
            h = g * (1.0 / (1.0 + jnp.exp(-g))) * u
            f = _bdot(h, weights[2][...])
            y = x + MACARON_WEIGHT * _rmsnorm(f, post_g_ref[...])
            if final_norm:
                y = _rmsnorm(y, final_g_ref[...])
            o_ref[rows, :] = y

    if not w_steps:
        row_tile()
        return
    i = pl.program_id(0)
    fc = wg_ref.shape[1]
    for c in range(w_steps):
        @pl.when(i == c)
        def _(c=c):
            cols = slice(c * fc, (c + 1) * fc)
            weights[0][:, cols] = wg_ref[...].astype(jnp.bfloat16)
            weights[1][:, cols] = wu_ref[...].astype(jnp.bfloat16)
            weights[2][cols, :] = wd_ref[...].astype(jnp.bfloat16)
    if rope_steps:
        pl.when(i < rope_steps)(lambda: _rope_table_block(pos_ref, invf_ref, cos_ref, sin_ref))
    pl.when(i >= w_steps)(row_tile)


def _ffn(x2d, pre_g, wg, wu, wd, post_g, final_g, final_norm, rope=None):
    n, d = x2d.shape
    f = wg.shape[1]
    tm = FFN_TM
    w_steps = 0 if wg.dtype == jnp.bfloat16 else FFN_W_STEPS
    rope_steps, rope_in, rope_specs, rope_out_specs, rope_shapes = 0, [], [], [], []
    if rope is not None:
        pos_rep, invf = rope
        tr = ROPE_TABLE_ROWS
        tok_per_row = LANES // (QK_ROPE_DIM // 2)
        rope_steps = pos_rep.shape[0] // tr
        assert 0 < rope_steps <= w_steps
        blk = lambda i: (jnp.minimum(i, rope_steps - 1), 0)
        rope_in = [pos_rep, invf]
        rope_specs = [pl.BlockSpec((tr, tok_per_row), blk), _resident(invf.shape)]
        rope_out_specs = [pl.BlockSpec((tr * tok_per_row, LANES), blk)] * 2
        rope_shapes = [jax.ShapeDtypeStruct((pos_rep.shape[0] * tok_per_row, LANES), jnp.float32)] * 2
    if w_steps:
        fc = f // w_steps
        chunk = lambda i: jnp.minimum(i, w_steps - 1)
        w_specs = [pl.BlockSpec((d, fc), lambda i: (0, chunk(i))), pl.BlockSpec((d, fc), lambda i: (0, chunk(i))),
                   pl.BlockSpec((fc, d), lambda i: (chunk(i), 0))]
        w_scratch = [pltpu.VMEM((d, f), jnp.bfloat16), pltpu.VMEM((d, f), jnp.bfloat16),
                     pltpu.VMEM((f, d), jnp.bfloat16)]
    else:
        w_specs, w_scratch = [_resident(wg.shape), _resident(wu.shape), _resident(wd.shape)], []
    row = pl.BlockSpec((tm, d), lambda i: (jnp.maximum(i - w_steps, 0), 0))
    outs = pl.pallas_call(
        functools.partial(_ffn_kernel, final_norm=final_norm, w_steps=w_steps, rope_steps=rope_steps),
        grid=(w_steps + n // tm,),
        in_specs=[row, _resident((1, d))] + w_specs + [_resident((1, d)), _resident((1, d))] + rope_specs,
        out_specs=[row] + rope_out_specs,
        out_shape=[jax.ShapeDtypeStruct((n, d), jnp.float32)] + rope_shapes,
        scratch_shapes=w_scratch,
        compiler_params=pltpu.CompilerParams(dimension_semantics=("arbitrary",), vmem_limit_bytes=VMEM_LIMIT),
        name="ffn_final" if final_norm else "ffn",
    )(x2d, pre_g, wg, wu, wd, post_g, final_g, *rope_in)
    return outs if rope_steps else outs[0]


def _cast_stream_specs(weights, n_steps, step_of):
    spec = lambda w: pl.BlockSpec((w.shape[0] // n_steps, w.shape[1]), lambda *g: (step_of(*g), 0))
    specs = [spec(w) for w in weights]
    return specs, specs, [jax.ShapeDtypeStruct(w.shape, jnp.bfloat16) for w in weights]


def _cast_stream_step(in_refs, out_refs):
    for src, dst in zip(in_refs, out_refs):
        dst[...] = src[...].astype(jnp.bfloat16)


def _extend_w_in(w_ref, wext_ref):
    kr1 = OFF_KR + QK_ROPE_DIM
    half = QK_ROPE_DIM // 2
    n_in = w_ref.shape[1]

    def body(rb, carry):
        rows = pl.ds(pl.multiple_of(rb * W_IN_ROWS, W_IN_ROWS), W_IN_ROWS)
        wext_ref[rows, 0:kr1] = w_ref[rows, 0:kr1].astype(jnp.bfloat16)
        wext_ref[rows, kr1:kr1 + half] = w_ref[rows, OFF_KR + half:kr1].astype(jnp.bfloat16)
        wext_ref[rows, kr1 + half:OFF_POOL] = w_ref[rows, OFF_KR:OFF_KR + half].astype(jnp.bfloat16)
        wext_ref[rows, OFF_POOL:IN_WIDTH_EXT] = w_ref[rows, kr1:n_in].astype(jnp.bfloat16)
        return carry

    jax.lax.fori_loop(0, w_ref.shape[0] // W_IN_ROWS, body, 0)


def _mixer_in_kernel(x_ref, cos_ref, sin_ref, pre_g_ref, w_in_ref, *refs):
    weights, (q_ref, k_ref, vt_ref, pool_ref, gate_ref), wext_ref = refs[:-6], refs[-6:-1], refs[-1]

    @pl.when((pl.program_id(0) == 0) & (pl.program_id(1) == 0))
    def _():
        _extend_w_in(w_in_ref, wext_ref)

    weights = (pre_g_ref, wext_ref) + tuple(weights)
    for r in range(x_ref.shape[1] // MIX_SUB):
        rows = pl.ds(r * MIX_SUB, MIX_SUB)
        _mixer_in_tile(x_ref.at[:, rows, :], cos_ref.at[:, rows, :], sin_ref.at[:, rows, :], *weights,
                       q_ref.at[:, :, rows, :], k_ref.at[:, :, rows, :], vt_ref.at[:, :, rows],
                       pool_ref.at[:, rows, :], gate_ref.at[:, rows, :])


def _mixer_in_tile(x_ref, cos_ref, sin_ref, pre_g_ref, w_in_ref, qg_ref, wq_ref, wqr_ref, kvg_ref, wuk_ref, wuvt_ref,
                   q_ref, k_ref, vt_ref, pool_ref, gate_ref):
    u = _rmsnorm(x_ref[0], pre_g_ref[...]).astype(jnp.bfloat16)
    z = jnp.dot(u, w_in_ref[:, :OFF_GATE], preferred_element_type=jnp.float32)
    pool_ref[0] = z[:, OFF_POOL:OFF_POOL + POOL_WIDTH]

    gl = jnp.dot(u, w_in_ref[:, OFF_GATE:], preferred_element_type=jnp.float32)
    gate_ref[0] = (1.0 / (1.0 + jnp.exp(-gl))).astype(jnp.bfloat16)

    cos = cos_ref[0]
    sin = sin_ref[0]
    lane = jax.lax.broadcasted_iota(jnp.int32, cos.shape, 1)
    first_half = (lane % QK_ROPE_DIM) < (QK_ROPE_DIM // 2)
    ssin = jnp.where(first_half, -sin, sin)

    kr2 = z[:, OFF_KR:OFF_KR + LANES] * jnp.where(lane < QK_ROPE_DIM, cos, ssin)
    k_rope = (kr2[:, :QK_ROPE_DIM] + kr2[:, QK_ROPE_DIM:]).astype(jnp.bfloat16)
    ckv = _rmsnorm(z[:, OFF_CKV:OFF_CKV + KV_LORA_RANK], kvg_ref[...]).astype(jnp.bfloat16)
    cq = _rmsnorm(z[:, OFF_CQ:OFF_CQ + Q_LORA_RANK], qg_ref[...]).astype(jnp.bfloat16)

    k_nope = jnp.dot(ckv, wuk_ref[...], preferred_element_type=jnp.float32).astype(jnp.bfloat16)
    for h in range(N_HEADS):
        k_ref[0, h, :, 0:QK_NOPE_DIM] = k_nope[:, h * QK_NOPE_DIM:(h + 1) * QK_NOPE_DIM]
        k_ref[0, h, :, QK_NOPE_DIM:QK_DIM] = k_rope
    vt_ref[0] = jax.lax.dot_general(wuvt_ref[...], ckv, (((1,), (1,)), ((), ())),
                                    preferred_element_type=jnp.float32).astype(jnp.bfloat16)

    scale = math.log2(math.e) / math.sqrt(QK_DIM)
    qr = jnp.dot(cq, wqr_ref[...], preferred_element_type=jnp.float32)
    n_rep = N_HEADS * QK_ROPE_DIM // LANES
    cos_h = jnp.concatenate([cos] * n_rep, axis=1)
    ssin_h = jnp.concatenate([ssin] * n_rep, axis=1)
    half = QK_ROPE_DIM // 2
    qs = jnp.where(jnp.concatenate([first_half] * n_rep, axis=1),
                   pltpu.roll(qr, qr.shape[1] - half, axis=1), pltpu.roll(qr, half, axis=1))
    q_rope = ((qr * cos_h + qs * ssin_h) * scale).astype(jnp.bfloat16)
    q_nope = (jnp.dot(cq, wq_ref[...], preferred_element_type=jnp.float32) * scale).astype(jnp.bfloat16)
    for h in range(N_HEADS):
        q_ref[0, h, :, 0:QK_NOPE_DIM] = q_nope[:, h * QK_NOPE_DIM:(h + 1) * QK_NOPE_DIM]
        q_ref[0, h, :, QK_NOPE_DIM:QK_DIM] = q_rope[:, h * QK_ROPE_DIM:(h + 1) * QK_ROPE_DIM]


def _mixer_in(x, cos, sin, pre_g, w_in, qg, wq, wqr, kvg, wuk, wuv):
    b, s, d = x.shape
    tm = MIX_IN_TM
    heads = lambda w: pl.BlockSpec((1, N_HEADS, tm, w), lambda bi, i: (bi, 0, i, 0))
    rows = lambda w: pl.BlockSpec((1, tm, w), lambda bi, i: (bi, i, 0))
    return pl.pallas_call(
        _mixer_in_kernel,
        grid=(b, s // tm),
        in_specs=[rows(d), rows(LANES), rows(LANES), _resident((1, d)),
                  pl.BlockSpec((None,) + w_in.shape[1:], lambda *_: (0, 0, 0), pipeline_mode=pl.Buffered(1)),
                  _resident(qg.shape),
                  _resident(wq.shape), _resident(wqr.shape), _resident(kvg.shape),
                  _resident(wuk.shape), _resident(wuv.shape)],
        out_specs=[heads(QK_DIM), heads(QK_DIM), pl.BlockSpec((1, N_HEADS * V_HEAD_DIM, tm), lambda bi, i: (bi, 0, i)), rows(POOL_WIDTH), rows(2 * D_MODEL)],
        out_shape=[jax.ShapeDtypeStruct((b, N_HEADS, s, QK_DIM), jnp.bfloat16),
                   jax.ShapeDtypeStruct((b, N_HEADS, s, QK_DIM), jnp.bfloat16),
                   jax.ShapeDtypeStruct((b, N_HEADS * V_HEAD_DIM, s), jnp.bfloat16),
                   jax.ShapeDtypeStruct((b, s, POOL_WIDTH), jnp.float32),
                   jax.ShapeDtypeStruct((b, s, 2 * D_MODEL), jnp.bfloat16)],
        scratch_shapes=[pltpu.VMEM((d, IN_WIDTH_EXT), jnp.bfloat16)],
        compiler_params=pltpu.CompilerParams(dimension_semantics=("arbitrary", "arbitrary"),
                                             vmem_limit_bytes=VMEM_LIMIT),
        name="mixer_in",
    )(x, cos, sin, pre_g, w_in, qg, wq, wqr, kvg, wuk, wuv)


def _attn_kernel(q_ref, k_ref, vt_ref, o_ref, st_ref, p_ref):
    n_heads, s_len = k_ref.shape[1], k_ref.shape[2]
    n_sub = s_len // ATT_TQ
    n_chunk = s_len // ATT_KC
    tiles = [(hh, j) for hh in range(n_heads) for j in range(n_sub)]

    def logits(t):
        hh, j = tiles[t]
        st = jax.lax.dot_general(k_ref[0, hh], q_ref[0, hh, j * ATT_TQ:(j + 1) * ATT_TQ, :],
                                 (((1,), (1,)), ((), ())), preferred_element_type=jnp.float32)
        st_ref[t % 2] = st
        return jnp.max(st, axis=0, keepdims=True)

    m_next = logits(0)
    for t, (hh, j) in enumerate(tiles):
        slot = t % 2
        m = m_next
        if t + 1 < len(tiles):
            m_next = logits(t + 1)
        l8 = jnp.zeros((SUBLANES, ATT_TQ), jnp.float32)
        for c in range(n_chunk):
            keys = slice(c * ATT_KC, (c + 1) * ATT_KC)
            p = jnp.exp2(st_ref[slot, keys, :] - m)
            l8 = l8 + jnp.sum(p.reshape(ATT_KC // SUBLANES, SUBLANES, ATT_TQ), axis=0)
            p_ref[slot, keys, :] = p.astype(jnp.bfloat16)
        l = jnp.sum(l8, axis=0, keepdims=True)
        ot = jnp.dot(vt_ref[0, hh], p_ref[slot], preferred_element_type=jnp.float32)
        o_ref[0, j * ATT_TQ:(j + 1) * ATT_TQ, hh * V_HEAD_DIM:(hh + 1) * V_HEAD_DIM] = (
            (ot / l).T.astype(jnp.bfloat16))


def _attention(q, k, vt):
    b, h, s, _ = q.shape
    hs = ATT_HEADS
    head = lambda r, c: pl.BlockSpec((1, hs, r, c), lambda bi, hi: (bi, hi, 0, 0))
    return pl.pallas_call(
        _attn_kernel,
        grid=(b, h // hs),
        in_specs=[head(s, QK_DIM), head(s, QK_DIM), head(V_HEAD_DIM, s)],
        out_specs=pl.BlockSpec((1, s, hs * V_HEAD_DIM), lambda bi, hi: (bi, 0, hi)),
        out_shape=jax.ShapeDtypeStruct((b, s, h * V_HEAD_DIM), jnp.bfloat16),
        scratch_shapes=[pltpu.VMEM((2, s, ATT_TQ), jnp.float32), pltpu.VMEM((2, s, ATT_TQ), jnp.bfloat16)],
        compiler_params=pltpu.CompilerParams(dimension_semantics=("arbitrary", "arbitrary"),
                                             vmem_limit_bytes=VMEM_LIMIT),
        name="attention",
    )(q, k, vt)


def _mixer_out_kernel(o_ref, pool_ref, prev_ref, next_ref, gate_ref, x_ref, wo_ref, pw_ref, ps_ref, wop_ref,
                      wout_ref, post_g_ref, *rest, seq_len):
    n_w = (len(rest) - 2) // 2
    out_ref, ext_ref = rest[n_w], rest[-1]
    _cast_stream_step(rest[:n_w], rest[n_w + 1:-1])
    i = pl.program_id(1)
    n_i = pl.num_programs(1)
    tm = pool_ref.shape[1]

    ext_ref[0:POOL_HALO, :] = jnp.where(i > 0, prev_ref[0], 0.0)
    ext_ref[POOL_HALO:POOL_HALO + tm, :] = pool_ref[0]
    ext_ref[POOL_HALO + tm:POOL_HALO + tm + POOL_HALO, :] = jnp.where(i < n_i - 1, next_ref[0], 0.0)

    win = MIX_SUB + 2 * POOL_HALO
    for r in range(tm // MIX_SUB):
        r0 = r * MIX_SUB
        rows = slice(r0, r0 + MIX_SUB)
        o_rows = o_ref[0, rows, :]
        t_first = i * tm + r0 + jax.lax.broadcasted_iota(jnp.int32, (POOL_HALO, 1), 0)
        t_last = t_first + (MIX_SUB - POOL_HALO)
        ys, y_attn_parts = [], []
        n_col = D_MODEL // len(POOL_WINDOWS)
        for gi, w in enumerate(POOL_WINDOWS):
            y_attn_parts.append(jnp.dot(o_rows, wo_ref[:, gi * n_col:(gi + 1) * n_col],
                                        preferred_element_type=jnp.float32))
            left = w // 2
            right = w - 1 - left
            cols = slice(gi * POOL_GROUP, (gi + 1) * POOL_GROUP)
            xe = ext_ref[r0:r0 + win, cols]
            fwd, span = xe, 1
            while span < min(w, POOL_HALO):
                fwd = fwd + pltpu.roll(fwd, win - span, axis=0)
                span *= 2
            if w > span:
                fwd = fwd + pltpu.roll(fwd, win - span, axis=0)
            total = pltpu.roll(fwd, left, axis=0)[POOL_HALO:POOL_HALO + MIX_SUB]
            inv = [1.0 / (jnp.minimum(tt + right + 1, seq_len) - jnp.maximum(tt - left, 0)).astype(jnp.float32)
                   for tt in (t_first, t_last)]
            mean = jnp.concatenate([total[:POOL_HALO] * inv[0], total[POOL_HALO:-POOL_HALO] * (1.0 / w),
                                    total[-POOL_HALO:] * inv[1]], axis=0)
            dg = mean - xe[POOL_HALO:POOL_HALO + MIX_SUB]
            ys.append(_bdot(dg, pw_ref[gi]))
        y = jnp.concatenate(ys, axis=1) * ps_ref[...]
        y_pool = _bdot(y, wop_ref[...])
        y_attn = jnp.concatenate(y_attn_parts, axis=1)

        g_attn = gate_ref[0, rows, 0:D_MODEL].astype(jnp.float32)
        g_pool = gate_ref[0, rows, D_MODEL:2 * D_MODEL].astype(jnp.float32)
        mixed = _bdot(g_attn * y_attn + g_pool * y_pool, wout_ref[...])
        out_ref[0, rows, :] = x_ref[0, rows, :] + _rmsnorm(mixed, post_g_ref[...])


def _mixer_out(o, pool, gate, x, wo, pw, ps, wop, wout, post_g, weights):
    b, s, d = x.shape
    tm = MIX_OUT_TM
    nb = tm // POOL_HALO
    last = s // POOL_HALO - 1
    n_i = s // tm
    w_in_specs, w_out_specs, w_shapes = _cast_stream_specs(weights, b * n_i, lambda bi, i: bi * n_i + i)
    rows = lambda w: pl.BlockSpec((1, tm, w), lambda bi, i: (bi, i, 0))
    prev = pl.BlockSpec((1, POOL_HALO, POOL_WIDTH), lambda bi, i: (bi, jnp.maximum(i * nb - 1, 0), 0))
    nxt = pl.BlockSpec((1, POOL_HALO, POOL_WIDTH), lambda bi, i: (bi, jnp.minimum((i + 1) * nb, last), 0))
    return pl.pallas_call(
        functools.partial(_mixer_out_kernel, seq_len=s),
        grid=(b, s // tm),
        in_specs=[rows(d), rows(POOL_WIDTH), prev, nxt, rows(2 * d), rows(d), _resident(wo.shape),
                  _resident(pw.shape), _resident(ps.shape), _resident(wop.shape), _resident(wout.shape),
                  _resident(post_g.shape)] + w_in_specs,
        out_specs=[rows(d)] + w_out_specs,
        out_shape=[jax.ShapeDtypeStruct((b, s, d), jnp.float32)] + w_shapes,
        scratch_shapes=[pltpu.VMEM((tm + 2 * POOL_HALO, POOL_WIDTH), jnp.float32)],
        compiler_params=pltpu.CompilerParams(dimension_semantics=("arbitrary", "arbitrary"),
                                             vmem_limit_bytes=VMEM_LIMIT),
        name="mixer_out",
    )(o, pool, pool, pool, gate, x, wo, pw, ps, wop, wout, post_g, *weights)


def _split_w_uq(w_uq):
    r = w_uq.shape[0]
    w3 = w_uq.reshape(r, N_HEADS, QK_DIM)
    nope = w3[:, :, :QK_NOPE_DIM].reshape(r, N_HEADS * QK_NOPE_DIM)
    return nope, w3[:, :, QK_NOPE_DIM:].reshape(r, N_HEADS * QK_ROPE_DIM)


def kernel(x, positions, ffn1_pre_g, ffn1_w_gate, ffn1_w_up, ffn1_w_down, ffn1_post_g, mix_pre_g, w_in, q_a_norm_g, w_uq, kv_a_norm_g, w_uk, w_uv, w_o_attn, pool_w, pool_scale, w_o_pool, w_out, mix_post_g, ffn2_pre_g, ffn2_w_gate, ffn2_w_up, ffn2_w_down, ffn2_post_g, final_g):
    b, s, d = x.shape
    bf = lambda w: w.astype(jnp.bfloat16)
    row = lambda g: g.reshape(1, -1)
    inv_freq = ROPE_THETA ** (-jnp.arange(0, QK_ROPE_DIM, 2, dtype=jnp.float32) / QK_ROPE_DIM)
    n_freq = QK_ROPE_DIM // 2
    invf = jnp.tile(inv_freq, LANES // n_freq).reshape(1, LANES)
    pos_rep = positions.reshape(b * s * n_freq // LANES, LANES // n_freq)
    assert w_in.shape[0] == 1, "one layer: the second FFN's weight casts are hosted by the mixer_out before it"
    l = 0

    x1, cos, sin = _ffn(x.reshape(b * s, d), row(ffn1_pre_g[l]), ffn1_w_gate[l], ffn1_w_up[l], ffn1_w_down[l],
                        row(ffn1_post_g[l]), row(final_g[l]), False, rope=(pos_rep, invf))
    x1, cos, sin = x1.reshape(b, s, d), cos.reshape(b, s, LANES), sin.reshape(b, s, LANES)
    wq, wqr = _split_w_uq(bf(w_uq[l]))
    q, k, vt, pool, gate = _mixer_in(
        x1, cos, sin, row(mix_pre_g[l]), w_in, row(q_a_norm_g[l]),
        wq, wqr, row(kv_a_norm_g[l]), bf(w_uk[l]), bf(w_uv[l]).T)
    o = _attention(q, k, vt.reshape(b, N_HEADS, V_HEAD_DIM, s))
    x2, *ffn2_w = _mixer_out(o, pool, gate, x1, bf(w_o_attn[l]), bf(pool_w[l]), row(pool_scale[l]),
                             bf(w_o_pool[l]), bf(w_out[l]), row(mix_post_g[l]),
                             [ffn2_w_gate[l], ffn2_w_up[l], ffn2_w_down[l]])
    return _ffn(x2.reshape(b * s, d), row(ffn2_pre_g[l]), *ffn2_w, row(ffn2_post_g[l]), row(final_g[l]),
                True).reshape(b, s, d)
```

```python
import functools
import math

import jax
import jax.numpy as jnp
from jax.experimental import pallas as pl
from jax.experimental.pallas import tpu as pltpu

D_MODEL = 1024
N_HEADS = 8
QK_NOPE_DIM = 128
QK_ROPE_DIM = 64
QK_DIM = QK_NOPE_DIM + QK_ROPE_DIM
V_HEAD_DIM = 128
Q_LORA_RANK = 384
KV_LORA_RANK = 256
ROPE_THETA = 10000.0
POOL_WINDOWS = (2, 4, 8, 16)
POOL_GROUP = 128
POOL_WIDTH = POOL_GROUP * len(POOL_WINDOWS)
D_FF = 2816
MACARON_WEIGHT = 0.5
NORM_EPS = 1e-6

LANES = 128
SUBLANES = 8
POOL_HALO = 8

OFF_CQ = 0
OFF_CKV = OFF_CQ + Q_LORA_RANK
OFF_KR = OFF_CKV + KV_LORA_RANK
OFF_POOL = OFF_KR + 2 * QK_ROPE_DIM
OFF_GATE = OFF_POOL + POOL_WIDTH
IN_WIDTH_EXT = OFF_GATE + 2 * D_MODEL
W_IN_ROWS = 64
FFN_W_STEPS = 11

FFN_TM = 1024
FFN_SUB = 256
MIX_IN_TM = 512
MIX_OUT_TM = 1024
MIX_SUB = 256
ATT_TQ = 512
ATT_KC = 256
ATT_HEADS = 4
VMEM_LIMIT = 56 * 1024 * 1024


def _resident(shape):
    return pl.BlockSpec(shape, lambda *_: (0,) * len(shape), pipeline_mode=pl.Buffered(1))


def _rmsnorm(x, g):
    ms = jnp.mean(x * x, axis=-1, keepdims=True)
    return x * jax.lax.rsqrt(ms + NORM_EPS) * g


def _bdot(a, b):
    return jnp.dot(a.astype(jnp.bfloat16), b, preferred_element_type=jnp.float32)


def _rope_table_block(pos_ref, invf_ref, cos_ref, sin_ref):
    ang = invf_ref[...] * pos_ref[0].astype(jnp.float32)
    n_copy = LANES // (QK_ROPE_DIM // 2)
    for table, out_ref in ((jnp.cos(ang), cos_ref), (jnp.sin(ang), sin_ref)):
        out_ref[...] = jnp.concatenate([table] * n_copy, axis=0).T


def _ffn_kernel(x_ref, pre_g_ref, wg_ref, wu_ref, wd_ref, post_g_ref, final_g_ref, *rest, final_norm, w_steps,
                rope_steps):
    if rope_steps:
        pos_ref, invf_ref, o_ref, cos_ref, sin_ref, *w_scratch = rest
    else:
        o_ref, *w_scratch = rest
    weights = w_scratch if w_steps else (wg_ref, wu_ref, wd_ref)

    def row_tile():
        for r in range(x_ref.shape[0] // FFN_SUB):
            rows = slice(r * FFN_SUB, (r + 1) * FFN_SUB)
            x = x_ref[rows, :]
            xn = _rmsnorm(x, pre_g_ref[...]).astype(jnp.bfloat16)
            g = jnp.dot(xn, weights[0][...], preferred_element_type=jnp.float32)
            u = jnp.dot(xn, weights[1][...], preferred_element_type=jnp.float32)
            h = g * (1.0 / (1.0 + jnp.exp(-g))) * u
            f = _bdot(h, weights[2][...])
            y = x + MACARON_WEIGHT * _rmsnorm(f, post_g_ref[...])
            if final_norm:
                y = _rmsnorm(y, final_g_ref[...])
            o_ref[rows, :] = y

    if not w_steps:
        row_tile()
        return
    i = pl.program_id(0)
    fc = wg_ref.shape[1]
    for c in range(w_steps):
        @pl.when(i == c)
        def _(c=c):
            cols = slice(c * fc, (c + 1) * fc)
            weights[0][:, cols] = wg_ref[...].astype(jnp.bfloat16)
            weights[1][:, cols] = wu_ref[...].astype(jnp.bfloat16)
            weights[2][cols, :] = wd_ref[...].astype(jnp.bfloat16)
    if rope_steps:
        pl.when(i < rope_steps)(lambda: _rope_table_block(pos_ref, invf_ref, cos_ref, sin_ref))
    pl.when(i >= w_steps)(row_tile)


def _ffn(x2d, pre_g, wg, wu, wd, post_g, final_g, final_norm, rope=None):
    n, d = x2d.shape
    f = wg.shape[1]
    tm = FFN_TM
    w_steps = 0 if wg.dtype == jnp.bfloat16 else FFN_W_STEPS
    rope_steps, rope_in, rope_specs, rope_out_specs, rope_shapes = 0, [], [], [], []
    if rope is not None:
        pos3, invf = rope
        rope_steps, seq = pos3.shape[0], pos3.shape[2]
        assert 0 < rope_steps <= w_steps
        blk = lambda i: (jnp.minimum(i, rope_steps - 1), 0)
        rope_in = [pos3, invf]
        rope_specs = [pl.BlockSpec((1, 1, seq), lambda i: (jnp.minimum(i, rope_steps - 1), 0, 0)),
                      _resident(invf.shape)]
        rope_out_specs = [pl.BlockSpec((seq, LANES), blk)] * 2
        rope_shapes = [jax.ShapeDtypeStruct((rope_steps * seq, LANES), jnp.float32)] * 2
    if w_steps:
        fc = f // w_steps
        chunk = lambda i: jnp.minimum(i, w_steps - 1)
        w_specs = [pl.BlockSpec((d, fc), lambda i: (0, chunk(i))), pl.BlockSpec((d, fc), lambda i: (0, chunk(i))),
                   pl.BlockSpec((fc, d), lambda i: (chunk(i), 0))]
        w_scratch = [pltpu.VMEM((d, f), jnp.bfloat16), pltpu.VMEM((d, f), jnp.bfloat16),
                     pltpu.VMEM((f, d), jnp.bfloat16)]
    else:
        w_specs, w_scratch = [_resident(wg.shape), _resident(wu.shape), _resident(wd.shape)], []
    row = pl.BlockSpec((tm, d), lambda i: (jnp.maximum(i - w_steps, 0), 0))
    outs = pl.pallas_call(
        functools.partial(_ffn_kernel, final_norm=final_norm, w_steps=w_steps, rope_steps=rope_steps),
        grid=(w_steps + n // tm,),
        in_specs=[row, _resident((1, d))] + w_specs + [_resident((1, d)), _resident((1, d))] + rope_specs,
        out_specs=[row] + rope_out_specs,
        out_shape=[jax.ShapeDtypeStruct((n, d), jnp.float32)] + rope_shapes,
        scratch_shapes=w_scratch,
        compiler_params=pltpu.CompilerParams(dimension_semantics=("arbitrary",), vmem_limit_bytes=VMEM_LIMIT),
        name="ffn_final" if final_norm else "ffn",
    )(x2d, pre_g, wg, wu, wd, post_g, final_g, *rope_in)
    return outs if rope_steps else outs[0]


def _cast_stream_specs(weights, n_steps, step_of):
    spec = lambda w: pl.BlockSpec((w.shape[0] // n_steps, w.shape[1]), lambda *g: (step_of(*g), 0))
    specs = [spec(w) for w in weights]
    return specs, specs, [jax.ShapeDtypeStruct(w.shape, jnp.bfloat16) for w in weights]


def _cast_stream_step(in_refs, out_refs):
    for src, dst in zip(in_refs, out_refs):
        dst[...] = src[...].astype(jnp.bfloat16)


def _extend_w_in(w_ref, wext_ref):
    kr1 = OFF_KR + QK_ROPE_DIM
    half = QK_ROPE_DIM // 2
    n_in = w_ref.shape[1]

    def body(rb, carry):
        rows = pl.ds(pl.multiple_of(rb * W_IN_ROWS, W_IN_ROWS), W_IN_ROWS)
        wext_ref[rows, 0:kr1] = w_ref[rows, 0:kr1].astype(jnp.bfloat16)
        wext_ref[rows, kr1:kr1 + half] = w_ref[rows, OFF_KR + half:kr1].astype(jnp.bfloat16)
        wext_ref[rows, kr1 + half:OFF_POOL] = w_ref[rows, OFF_KR:OFF_KR + half].astype(jnp.bfloat16)
        wext_ref[rows, OFF_POOL:IN_WIDTH_EXT] = w_ref[rows, kr1:n_in].astype(jnp.bfloat16)
        return carry

    jax.lax.fori_loop(0, w_ref.shape[0] // W_IN_ROWS, body, 0)


def _mixer_in_kernel(x_ref, cos_ref, sin_ref, pre_g_ref, w_in_ref, *refs):
    weights, (q_ref, k_ref, vt_ref, pool_ref, gate_ref), wext_ref = refs[:-6], refs[-6:-1], refs[-1]

    @pl.when((pl.program_id(0) == 0) & (pl.program_id(1) == 0))
    def _():
        _extend_w_in(w_in_ref, wext_ref)

    weights = (pre_g_ref, wext_ref) + tuple(weights)
    for r in range(x_ref.shape[1] // MIX_SUB):
        rows = pl.ds(r * MIX_SUB, MIX_SUB)
        _mixer_in_tile(x_ref.at[:, rows, :], cos_ref.at[:, rows, :], sin_ref.at[:, rows, :], *weights,
                       q_ref.at[:, :, rows, :], k_ref.at[:, :, rows, :], vt_ref.at[:, :, rows],
                       pool_ref.at[:, rows, :], gate_ref.at[:, rows, :])


def _mixer_in_tile(x_ref, cos_ref, sin_ref, pre_g_ref, w_in_ref, qg_ref, wq_ref, wqr_ref, kvg_ref, wuk_ref, wuvt_ref,
                   q_ref, k_ref, vt_ref, pool_ref, gate_ref):
    u = _rmsnorm(x_ref[0], pre_g_ref[...]).astype(jnp.bfloat16)
    z = jnp.dot(u, w_in_ref[:, :OFF_GATE], preferred_element_type=jnp.float32)
    pool_ref[0] = z[:, OFF_POOL:OFF_POOL + POOL_WIDTH]

    gl = jnp.dot(u, w_in_ref[:, OFF_GATE:], preferred_element_type=jnp.float32)
    gate_ref[0] = (1.0 / (1.0 + jnp.exp(-gl))).astype(jnp.bfloat16)

    cos = cos_ref[0]
    sin = sin_ref[0]
    lane = jax.lax.broadcasted_iota(jnp.int32, cos.shape, 1)
    first_half = (lane % QK_ROPE_DIM) < (QK_ROPE_DIM // 2)
    ssin = jnp.where(first_half, -sin, sin)

    kr2 = z[:, OFF_KR:OFF_KR + LANES] * jnp.where(lane < QK_ROPE_DIM, cos, ssin)
    k_rope = (kr2[:, :QK_ROPE_DIM] + kr2[:, QK_ROPE_DIM:]).astype(jnp.bfloat16)
    ckv = _rmsnorm(z[:, OFF_CKV:OFF_CKV + KV_LORA_RANK], kvg_ref[...]).astype(jnp.bfloat16)
    cq = _rmsnorm(z[:, OFF_CQ:OFF_CQ + Q_LORA_RANK], qg_ref[...]).astype(jnp.bfloat16)

    k_nope = jnp.dot(ckv, wuk_ref[...], preferred_element_type=jnp.float32).astype(jnp.bfloat16)
    for h in range(N_HEADS):
        k_ref[0, h, :, 0:QK_NOPE_DIM] = k_nope[:, h * QK_NOPE_DIM:(h + 1) * QK_NOPE_DIM]
        k_ref[0, h, :, QK_NOPE_DIM:QK_DIM] = k_rope
    vt_ref[0] = jax.lax.dot_general(wuvt_ref[...], ckv, (((1,), (1,)), ((), ())),
                                    preferred_element_type=jnp.float32).astype(jnp.bfloat16)

    scale = math.log2(math.e) / math.sqrt(QK_DIM)
    qr = jnp.dot(cq, wqr_ref[...], preferred_element_type=jnp.float32)
    n_rep = N_HEADS * QK_ROPE_DIM // LANES
    cos_h = jnp.concatenate([cos] * n_rep, axis=1)
    ssin_h = jnp.concatenate([ssin] * n_rep, axis=1)
    half = QK_ROPE_DIM // 2
    qs = jnp.where(jnp.concatenate([first_half] * n_rep, axis=1),
                   pltpu.roll(qr, qr.shape[1] - half, axis=1), pltpu.roll(qr, half, axis=1))
    q_rope = ((qr * cos_h + qs * ssin_h) * scale).astype(jnp.bfloat16)
    q_nope = (jnp.dot(cq, wq_ref[...], preferred_element_type=jnp.float32) * scale).astype(jnp.bfloat16)
    for h in range(N_HEADS):
        q_ref[0, h, :, 0:QK_NOPE_DIM] = q_nope[:, h * QK_NOPE_DIM:(h + 1) * QK_NOPE_DIM]
        q_ref[0, h, :, QK_NOPE_DIM:QK_DIM] = q_rope[:, h * QK_ROPE_DIM:(h + 1) * QK_ROPE_DIM]


def _mixer_in(x, cos, sin, pre_g, w_in, qg, wq, wqr, kvg, wuk, wuv):
    b, s, d = x.shape
    tm = MIX_IN_TM
    heads = lambda w: pl.BlockSpec((1, N_HEADS, tm, w), lambda bi, i: (bi, 0, i, 0))
    rows = lambda w: pl.BlockSpec((1, tm, w), lambda bi, i: (bi, i, 0))
    return pl.pallas_call(
        _mixer_in_kernel,
        grid=(b, s // tm),
        in_specs=[rows(d), rows(LANES), rows(LANES), _resident((1, d)),
                  pl.BlockSpec((None,) + w_in.shape[1:], lambda *_: (0, 0, 0), pipeline_mode=pl.Buffered(1)),
                  _resident(qg.shape),
                  _resident(wq.shape), _resident(wqr.shape), _resident(kvg.shape),
                  _resident(wuk.shape), _resident(wuv.shape)],
        out_specs=[heads(QK_DIM), heads(QK_DIM), pl.BlockSpec((1, N_HEADS * V_HEAD_DIM, tm), lambda bi, i: (bi, 0, i)), rows(POOL_WIDTH), rows(2 * D_MODEL)],
        out_shape=[jax.ShapeDtypeStruct((b, N_HEADS, s, QK_DIM), jnp.bfloat16),
                   jax.ShapeDtypeStruct((b, N_HEADS, s, QK_DIM), jnp.bfloat16),
                   jax.ShapeDtypeStruct((b, N_HEADS * V_HEAD_DIM, s), jnp.bfloat16),
                   jax.ShapeDtypeStruct((b, s, POOL_WIDTH), jnp.float32),
                   jax.ShapeDtypeStruct((b, s, 2 * D_MODEL), jnp.bfloat16)],
        scratch_shapes=[pltpu.VMEM((d, IN_WIDTH_EXT), jnp.bfloat16)],
        compiler_params=pltpu.CompilerParams(dimension_semantics=("arbitrary", "arbitrary"),
                                             vmem_limit_bytes=VMEM_LIMIT),
        name="mixer_in",
    )(x, cos, sin, pre_g, w_in, qg, wq, wqr, kvg, wuk, wuv)


def _attn_kernel(q_ref, k_ref, vt_ref, o_ref, st_ref, p_ref):
    n_heads, s_len = k_ref.shape[1], k_ref.shape[2]
    n_sub = s_len // ATT_TQ
    n_chunk = s_len // ATT_KC
    tiles = [(hh, j) for hh in range(n_heads) for j in range(n_sub)]

    def logits(t):
        hh, j = tiles[t]
        st = jax.lax.dot_general(k_ref[0, hh], q_ref[0, hh, j * ATT_TQ:(j + 1) * ATT_TQ, :],
                                 (((1,), (1,)), ((), ())), preferred_element_type=jnp.float32)
        st_ref[t % 2] = st
        return jnp.max(st, axis=0, keepdims=True)

    m_next = logits(0)
    for t, (hh, j) in enumerate(tiles):
        slot = t % 2
        m = m_next
        if t + 1 < len(tiles):
            m_next = logits(t + 1)
        l8 = jnp.zeros((SUBLANES, ATT_TQ), jnp.float32)
        for c in range(n_chunk):
            keys = slice(c * ATT_KC, (c + 1) * ATT_KC)
            p = jnp.exp2(st_ref[slot, keys, :] - m)
            l8 = l8 + jnp.sum(p.reshape(ATT_KC // SUBLANES, SUBLANES, ATT_TQ), axis=0)
            p_ref[slot, keys, :] = p.astype(jnp.bfloat16)
        l = jnp.sum(l8, axis=0, keepdims=True)
        ot = jnp.dot(vt_ref[0, hh], p_ref[slot], preferred_element_type=jnp.float32)
        o_ref[0, j * ATT_TQ:(j + 1) * ATT_TQ, hh * V_HEAD_DIM:(hh + 1) * V_HEAD_DIM] = (
            (ot / l).T.astype(jnp.bfloat16))


def _attention(q, k, vt):
    b, h, s, _ = q.shape
    hs = ATT_HEADS
    head = lambda r, c: pl.BlockSpec((1, hs, r, c), lambda bi, hi: (bi, hi, 0, 0))
    return pl.pallas_call(
        _attn_kernel,
        grid=(b, h // hs),
        in_specs=[head(s, QK_DIM), head(s, QK_DIM), head(V_HEAD_DIM, s)],
        out_specs=pl.BlockSpec((1, s, hs * V_HEAD_DIM), lambda bi, hi: (bi, 0, hi)),
        out_shape=jax.ShapeDtypeStruct((b, s, h * V_HEAD_DIM), jnp.bfloat16),
        scratch_shapes=[pltpu.VMEM((2, s, ATT_TQ), jnp.float32), pltpu.VMEM((2, s, ATT_TQ), jnp.bfloat16)],
        compiler_params=pltpu.CompilerParams(dimension_semantics=("arbitrary", "arbitrary"),
                                             vmem_limit_bytes=VMEM_LIMIT),
        name="attention",
    )(q, k, vt)


def _mixer_out_kernel(o_ref, pool_ref, prev_ref, next_ref, gate_ref, x_ref, wo_ref, pw_ref, ps_ref, wop_ref,
                      wout_ref, post_g_ref, *rest, seq_len):
    n_w = (len(rest) - 2) // 2
    out_ref, ext_ref = rest[n_w], rest[-1]
    _cast_stream_step(rest[:n_w], rest[n_w + 1:-1])
    i = pl.program_id(1)
    n_i = pl.num_programs(1)
    tm = pool_ref.shape[1]

    ext_ref[0:POOL_HALO, :] = jnp.where(i > 0, prev_ref[0], 0.0)
    ext_ref[POOL_HALO:POOL_HALO + tm, :] = pool_ref[0]
    ext_ref[POOL_HALO + tm:POOL_HALO + tm + POOL_HALO, :] = jnp.where(i < n_i - 1, next_ref[0], 0.0)

    win = MIX_SUB + 2 * POOL_HALO
    for r in range(tm // MIX_SUB):
        r0 = r * MIX_SUB
        rows = slice(r0, r0 + MIX_SUB)
        o_rows = o_ref[0, rows, :]
        t_first = i * tm + r0 + jax.lax.broadcasted_iota(jnp.int32, (POOL_HALO, 1), 0)
        t_last = t_first + (MIX_SUB - POOL_HALO)
        ys, y_attn_parts = [], []
        n_col = D_MODEL // len(POOL_WINDOWS)
        for gi, w in enumerate(POOL_WINDOWS):
            y_attn_parts.append(jnp.dot(o_rows, wo_ref[:, gi * n_col:(gi + 1) * n_col],
                                        preferred_element_type=jnp.float32))
            left = w // 2
            right = w - 1 - left
            cols = slice(gi * POOL_GROUP, (gi + 1) * POOL_GROUP)
            xe = ext_ref[r0:r0 + win, cols]
            fwd, span = xe, 1
            while span < min(w, POOL_HALO):
                fwd = fwd + pltpu.roll(fwd, win - span, axis=0)
                span *= 2
            if w > span:
                fwd = fwd + pltpu.roll(fwd, win - span, axis=0)
            total = pltpu.roll(fwd, left, axis=0)[POOL_HALO:POOL_HALO + MIX_SUB]
            inv = [1.0 / (jnp.minimum(tt + right + 1, seq_len) - jnp.maximum(tt - left, 0)).astype(jnp.float32)
                   for tt in (t_first, t_last)]
            mean = jnp.concatenate([total[:POOL_HALO] * inv[0], total[POOL_HALO:-POOL_HALO] * (1.0 / w),
                                    total[-POOL_HALO:] * inv[1]], axis=0)
            dg = mean - xe[POOL_HALO:POOL_HALO + MIX_SUB]
            ys.append(_bdot(dg, pw_ref[gi]))
        y = jnp.concatenate(ys, axis=1) * ps_ref[...]
        y_pool = _bdot(y, wop_ref[...])
        y_attn = jnp.concatenate(y_attn_parts, axis=1)

        g_attn = gate_ref[0, rows, 0:D_MODEL].astype(jnp.float32)
        g_pool = gate_ref[0, rows, D_MODEL:2 * D_MODEL].astype(jnp.float32)
        mixed = _bdot(g_attn * y_attn + g_pool * y_pool, wout_ref[...])
        out_ref[0, rows, :] = x_ref[0, rows, :] + _rmsnorm(mixed, post_g_ref[...])


def _mixer_out(o, pool, gate, x, wo, pw, ps, wop, wout, post_g, weights):
    b, s, d = x.shape
    tm = MIX_OUT_TM
    nb = tm // POOL_HALO
    last = s // POOL_HALO - 1
    n_i = s // tm
    w_in_specs, w_out_specs, w_shapes = _cast_stream_specs(weights, b * n_i, lambda bi, i: bi * n_i + i)
    rows = lambda w: pl.BlockSpec((1, tm, w), lambda bi, i: (bi, i, 0))
    prev = pl.BlockSpec((1, POOL_HALO, POOL_WIDTH), lambda bi, i: (bi, jnp.maximum(i * nb - 1, 0), 0))
    nxt = pl.BlockSpec((1, POOL_HALO, POOL_WIDTH), lambda bi, i: (bi, jnp.minimum((i + 1) * nb, last), 0))
    return pl.pallas_call(
        functools.partial(_mixer_out_kernel, seq_len=s),
        grid=(b, s // tm),
        in_specs=[rows(d), rows(POOL_WIDTH), prev, nxt, rows(2 * d), rows(d), _resident(wo.shape),
                  _resident(pw.shape), _resident(ps.shape), _resident(wop.shape), _resident(wout.shape),
                  _resident(post_g.shape)] + w_in_specs,
        out_specs=[rows(d)] + w_out_specs,
        out_shape=[jax.ShapeDtypeStruct((b, s, d), jnp.float32)] + w_shapes,
        scratch_shapes=[pltpu.VMEM((tm + 2 * POOL_HALO, POOL_WIDTH), jnp.float32)],
        compiler_params=pltpu.CompilerParams(dimension_semantics=("arbitrary", "arbitrary"),
                                             vmem_limit_bytes=VMEM_LIMIT),
        name="mixer_out",
    )(o, pool, pool, pool, gate, x, wo, pw, ps, wop, wout, post_g, *weights)


def _split_w_uq(w_uq):
    r = w_uq.shape[0]
    w3 = w_uq.reshape(r, N_HEADS, QK_DIM)
    nope = w3[:, :, :QK_NOPE_DIM].reshape(r, N_HEADS * QK_NOPE_DIM)
    return nope, w3[:, :, QK_NOPE_DIM:].reshape(r, N_HEADS * QK_ROPE_DIM)


def kernel(x, positions, ffn1_pre_g, ffn1_w_gate, ffn1_w_up, ffn1_w_down, ffn1_post_g, mix_pre_g, w_in, q_a_norm_g, w_uq, kv_a_norm_g, w_uk, w_uv, w_o_attn, pool_w, pool_scale, w_o_pool, w_out, mix_post_g, ffn2_pre_g, ffn2_w_gate, ffn2_w_up, ffn2_w_down, ffn2_post_g, final_g):
    b, s, d = x.shape
    bf = lambda w: w.astype(jnp.bfloat16)
    row = lambda g: g.reshape(1, -1)
    inv_freq = ROPE_THETA ** (-jnp.arange(0, QK_ROPE_DIM, 2, dtype=jnp.float32) / QK_ROPE_DIM)
    invf = inv_freq.reshape(-1, 1)
    pos_rep = positions.reshape(b, 1, s)
    assert w_in.shape[0] == 1, "one layer: the second FFN's weight casts are hosted by the mixer_out before it"
    l = 0

    x1, cos, sin = _ffn(x.reshape(b * s, d), row(ffn1_pre_g[l]), ffn1_w_gate[l], ffn1_w_up[l], ffn1_w_down[l],
                        row(ffn1_post_g[l]), row(final_g[l]), False, rope=(pos_rep, invf))
    x1, cos, sin = x1.reshape(b, s, d), cos.reshape(b, s, LANES), sin.reshape(b, s, LANES)
    wq, wqr = _split_w_uq(bf(w_uq[l]))
    q, k, vt, pool, gate = _mixer_in(
        x1, cos, sin, row(mix_pre_g[l]), w_in, row(q_a_norm_g[l]),
        wq, wqr, row(kv_a_norm_g[l]), bf(w_uk[l]), bf(w_uv[l]).T)
    o = _attention(q, k, vt.reshape(b, N_HEADS, V_HEAD_DIM, s))
    x2, *ffn2_w = _mixer_out(o, pool, gate, x1, bf(w_o_attn[l]), bf(pool_w[l]), row(pool_scale[l]),
                             bf(w_o_pool[l]), bf(w_out[l]), row(mix_post_g[l]),
                             [ffn2_w_gate[l], ffn2_w_up[l], ffn2_w_down[l]])
    return _ffn(x2.reshape(b * s, d), row(ffn2_pre_g[l]), *ffn2_w, row(ffn2_post_g[l]), row(final_g[l]),
                True).reshape(b, s, d)
```
